```python
import jax, jax.numpy as jnp
from jax import lax
import numpy as np

D_MODEL = 2048
BATCH = 1
SEQ = 8192
DEPTH = 2

GRID_W = 64
CTX_LEN = 256
HEAD_DIM = 128
N_Q_HEADS = D_MODEL // HEAD_DIM
N_KV_HEADS = N_Q_HEADS // 4
GQA_GROUP = N_Q_HEADS // N_KV_HEADS
WINDOW = 128
BLOCK = 128
ROPE_AXIS_DIM = HEAD_DIM // 2
ROPE_PAIRS = ROPE_AXIS_DIM // 2
ROPE_BASE = 10000.0
POOL_WINDOWS = (2, 4, 8, 16)
POOL_WIDTH = D_MODEL // 2
POOL_GROUP = POOL_WIDTH // len(POOL_WINDOWS)
CONV_WIDTH = D_MODEL // 2
N_BRANCHES = 3
D_FF = 256 * ((8 * D_MODEL // 3 + 255) // 256)
N_MODS = 9
EPS = 1e-6
NEG_INF = -1e30

ATTN_WIDTH = N_Q_HEADS * HEAD_DIM
KV_WIDTH = N_KV_HEADS * HEAD_DIM
Q_OFF = 0
K_OFF = Q_OFF + ATTN_WIDTH
V_OFF = K_OFF + KV_WIDTH
POOL_OFF = V_OFF + KV_WIDTH
CB_OFF = POOL_OFF + POOL_WIDTH
CC_OFF = CB_OFF + CONV_WIDTH
CX_OFF = CC_OFF + CONV_WIDTH
GATE_OFF = CX_OFF + CONV_WIDTH
IN_COLS = GATE_OFF + N_BRANCHES * D_MODEL

kernel_name = "hybrid_pool_conv_swa_dit_block"


def rms_norm(x, g):
    xf = x.astype(jnp.float32)
    y = xf * lax.rsqrt(jnp.mean(xf * xf, axis=-1, keepdims=True) + EPS)
    return (y * g.astype(jnp.float32)).astype(x.dtype)


def modulate(xn, shift, scale):
    return xn * (1 + scale) + shift


def swiglu(x, wi, wo):
    g, u = jnp.split(x @ wi, 2, axis=-1)
    return (jax.nn.silu(g) * u) @ wo


def axial_rope_tables(L):
    rows = L // GRID_W
    r = jnp.repeat(jnp.arange(rows, dtype=jnp.float32), GRID_W)
    col = jnp.tile(jnp.arange(GRID_W, dtype=jnp.float32), rows)
    inv = ROPE_BASE ** (-jnp.arange(ROPE_PAIRS, dtype=jnp.float32) / ROPE_PAIRS)
    ang = jnp.stack([r[:, None] * inv, col[:, None] * inv], axis=1)
    return jnp.cos(ang), jnp.sin(ang)


def apply_axial_rope(t, cos, sin):
    B, L, H, _ = t.shape
    tt = t.reshape(B, L, H, 2, ROPE_AXIS_DIM)
    t1, t2 = tt[..., :ROPE_PAIRS], tt[..., ROPE_PAIRS:]
    c = cos[None, :, None].astype(t.dtype)
    s = sin[None, :, None].astype(t.dtype)
    out = jnp.concatenate([t1 * c - t2 * s, t2 * c + t1 * s], axis=-1)
    return out.reshape(B, L, H, HEAD_DIM)


def windowed_attention(q, k, v, kc, vc, sink):
    B, L = q.shape[0], q.shape[1]
    nb = L // BLOCK
    qb = q.reshape(B, nb, BLOCK, N_KV_HEADS, GQA_GROUP, HEAD_DIM)

    def band(t):
        tb = t.reshape(B, nb, BLOCK, N_KV_HEADS, HEAD_DIM)
        tp = jnp.pad(tb, ((0, 0), (1, 1), (0, 0), (0, 0), (0, 0)))
        return jnp.concatenate([tp[:, :-2], tp[:, 1:-1], tp[:, 2:]], axis=2)

    kb, vb = band(k), band(v)
    scale = HEAD_DIM ** -0.5
    s_loc = jnp.einsum('bnqhgd,bnshd->bnhgqs', qb, kb).astype(jnp.float32) * scale
    s_ctx = jnp.einsum('bnqhgd,bchd->bnhgqc', qb, kc).astype(jnp.float32) * scale
    blk = jnp.arange(nb)[:, None, None]
    qpos = blk * BLOCK + jnp.arange(BLOCK)[None, :, None]
    kpos = (blk - 1) * BLOCK + jnp.arange(3 * BLOCK)[None, None, :]
    valid = (jnp.abs(kpos - qpos) <= WINDOW) & (kpos >= 0) & (kpos < L)
    s_loc = jnp.where(valid[None, :, None, None], s_loc, NEG_INF)
    s_sink = jnp.broadcast_to(
        sink.astype(jnp.float32).reshape(1, 1, N_KV_HEADS, GQA_GROUP, 1, 1), s_loc.shape[:-1] + (1,))
    p = jax.nn.softmax(jnp.concatenate([s_loc, s_ctx, s_sink], axis=-1), axis=-1)
    n_loc = 3 * BLOCK
    p_loc = p[..., :n_loc].astype(v.dtype)
    p_ctx = p[..., n_loc:n_loc + kc.shape[1]].astype(v.dtype)
    o = (jnp.einsum('bnhgqs,bnshd->bnqhgd', p_loc, vb)
         + jnp.einsum('bnhgqc,bchd->bnqhgd', p_ctx, vc))
    return o.reshape(B, L, ATTN_WIDTH)


def context_attention(q, k, v, sink):
    B, Lc = q.shape[0], q.shape[1]
    qg = q.reshape(B, Lc, N_KV_HEADS, GQA_GROUP, HEAD_DIM)
    s = jnp.einsum('bqhgd,bkhd->bhgqk', qg, k).astype(jnp.float32) * (HEAD_DIM ** -0.5)
    s_sink = jnp.broadcast_to(
        sink.astype(jnp.float32).reshape(1, N_KV_HEADS, GQA_GROUP, 1, 1), s.shape[:-1] + (1,))
    p = jax.nn.softmax(jnp.concatenate([s, s_sink], axis=-1), axis=-1)[..., :-1].astype(v.dtype)
    o = jnp.einsum('bhgqk,bkhd->bqhgd', p, v)
    return o.reshape(B, Lc, ATTN_WIDTH)


def multiscale_pool(u, pool_w, pool_scale):
    B, L, _ = u.shape
    uf = u.astype(jnp.float32)
    cs = jnp.concatenate([jnp.zeros((B, 1, POOL_WIDTH), jnp.float32), jnp.cumsum(uf, axis=1)], axis=1)
    t = jnp.arange(L)
    means = []
    for gi, w in enumerate(POOL_WINDOWS):
        lo = jnp.clip(t - w // 2, 0, L - 1)
        hi = jnp.clip(t + (w - w // 2) - 1, 0, L - 1)
        seg = cs[:, :, gi * POOL_GROUP:(gi + 1) * POOL_GROUP]
        cnt = (hi - lo + 1).astype(jnp.float32)[None, :, None]
        means.append((seg[:, hi + 1] - seg[:, lo]) / cnt)
    y = (jnp.concatenate(means, axis=-1) - uf).astype(u.dtype)
    y = jnp.einsum('blgc,gcd->blgd', y.reshape(B, L, len(POOL_WINDOWS), POOL_GROUP), pool_w)
    return y.reshape(B, L, POOL_WIDTH) * pool_scale


def short_conv(v, w):
    vp = jnp.pad(v, ((0, 0), (1, 1), (0, 0)))
    return vp[:, :-2] * w[0] + vp[:, 1:-1] * w[1] + vp[:, 2:] * w[2]


def parallel_merge(p, attn_o, pool_w, pool_scale, conv_w, w_attn_out, w_pool_out, w_conv_out, w_o):
    y_attn = attn_o @ w_attn_out
    y_pool = multiscale_pool(p[..., POOL_OFF:CB_OFF], pool_w, pool_scale) @ w_pool_out
    b_gate = p[..., CB_OFF:CC_OFF]
    c_gate = p[..., CC_OFF:CX_OFF]
    xv = p[..., CX_OFF:GATE_OFF]
    y_conv = (b_gate * short_conv(c_gate * xv, conv_w)) @ w_conv_out
    g = jax.nn.sigmoid(p[..., GATE_OFF:IN_COLS].astype(jnp.float32)).astype(p.dtype)
    g = g.reshape(p.shape[:-1] + (N_BRANCHES, D_MODEL))
    merged = g[..., 0, :] * y_attn + g[..., 1, :] * y_pool + g[..., 2, :] * y_conv
    return merged @ w_o


def setup_inputs(seed: int = 0) -> dict:
    key = jax.random.key(seed)
    ks = jax.random.split(key, 24)

    def dense(k, shape, fan_in, gain=1.0):
        return jax.random.normal(k, shape, jnp.float32) * (gain * fan_in ** -0.5)

    def near_one(k, shape, s):
        return 1.0 + s * jax.random.normal(k, shape, jnp.float32)

    D, F = D_MODEL, D_FF
    return {
        "x": jax.random.normal(ks[0], (BATCH, SEQ, D), jnp.float32),
        "c": jax.random.normal(ks[1], (BATCH, D), jnp.float32),
        "ctx": jax.random.normal(ks[2], (BATCH, CTX_LEN, D), jnp.float32),
        "c_ctx": jax.random.normal(ks[3], (D,), jnp.float32),
        "w_ada": dense(ks[4], (DEPTH, D, N_MODS * D), D, 0.5),
        "b_ada": 0.01 * jax.random.normal(ks[5], (DEPTH, N_MODS * D), jnp.float32),
        "norm_g": near_one(ks[6], (DEPTH, 3, D), 0.05),
        "ffn1_wi": dense(ks[7], (DEPTH, D, 2 * F), D),
        "ffn1_wo": dense(ks[8], (DEPTH, F, D), F),
        "w_in": dense(ks[9], (DEPTH, D, IN_COLS), D),
        "attn_sink": 0.5 * jax.random.normal(ks[10], (DEPTH, N_Q_HEADS), jnp.float32),
        "pool_w": dense(ks[11], (DEPTH, len(POOL_WINDOWS), POOL_GROUP, POOL_GROUP), POOL_GROUP),
        "pool_scale": near_one(ks[12], (DEPTH, POOL_WIDTH), 0.1),
        "conv_w": dense(ks[13], (DEPTH, 3, CONV_WIDTH), 3),
        "w_attn_out": dense(ks[14], (DEPTH, ATTN_WIDTH, D), ATTN_WIDTH),
        "w_pool_out": dense(ks[15], (DEPTH, POOL_WIDTH, D), POOL_WIDTH),
        "w_conv_out": dense(ks[16], (DEPTH, CONV_WIDTH, D), CONV_WIDTH),
        "w_o": dense(ks[17], (DEPTH, D, D), D),
        "ffn2_wi": dense(ks[18], (DEPTH, D, 2 * F), D),
        "ffn2_wo": dense(ks[19], (DEPTH, F, D), F),
        "final_g": near_one(ks[20], (D,), 0.05),
    }


def reference(x, c, ctx, c_ctx, w_ada, b_ada, norm_g, ffn1_wi, ffn1_wo, w_in, attn_sink, pool_w,
              pool_scale, conv_w, w_attn_out, w_pool_out, w_conv_out, w_o, ffn2_wi, ffn2_wo, final_g):
    B, L, D = x.shape
    Lc = ctx.shape[1]
    cos, sin = axial_rope_tables(L)
    h, hc = x, ctx
    for l in range(DEPTH):
        last = l == DEPTH - 1
        m = (jax.nn.silu(c) @ w_ada[l] + b_ada[l]).reshape(B, N_MODS, 1, D)
        mc = (jax.nn.silu(c_ctx) @ w_ada[l] + b_ada[l]).reshape(N_MODS, 1, 1, D)

        h = h + 0.5 * m[:, 2] * swiglu(
            modulate(rms_norm(h, norm_g[l, 0]), m[:, 0], m[:, 1]), ffn1_wi[l], ffn1_wo[l])
        hc = hc + 0.5 * mc[2] * swiglu(
            modulate(rms_norm(hc, norm_g[l, 0]), mc[0], mc[1]), ffn1_wi[l], ffn1_wo[l])

        xn = modulate(rms_norm(h, norm_g[l, 1]), m[:, 3], m[:, 4])
        xc = modulate(rms_norm(hc, norm_g[l, 1]), mc[3], mc[4])
        if last:
            kv_c = xc @ w_in[l][:, K_OFF:POOL_OFF]
        else:
            pc = xc @ w_in[l]
            kv_c = pc[..., K_OFF:POOL_OFF]
        kc = kv_c[..., :KV_WIDTH].reshape(B, Lc, N_KV_HEADS, HEAD_DIM)
        vc = kv_c[..., KV_WIDTH:].reshape(B, Lc, N_KV_HEADS, HEAD_DIM)

        p = xn @ w_in[l]
        q = apply_axial_rope(p[..., Q_OFF:K_OFF].reshape(B, L, N_Q_HEADS, HEAD_DIM), cos, sin)
        k = apply_axial_rope(p[..., K_OFF:V_OFF].reshape(B, L, N_KV_HEADS, HEAD_DIM), cos, sin)
        v = p[..., V_OFF:POOL_OFF].reshape(B, L, N_KV_HEADS, HEAD_DIM)
        attn = windowed_attention(q, k, v, kc, vc, attn_sink[l])
        h = h + m[:, 5] * parallel_merge(p, attn, pool_w[l], pool_scale[l], conv_w[l],
                                         w_attn_out[l], w_pool_out[l], w_conv_out[l], w_o[l])
        if not last:
            qc = pc[..., Q_OFF:K_OFF].reshape(B, Lc, N_Q_HEADS, HEAD_DIM)
            attn_c = context_attention(qc, kc, vc, attn_sink[l])
            hc = hc + mc[5] * parallel_merge(pc, attn_c, pool_w[l], pool_scale[l], conv_w[l],
                                             w_attn_out[l], w_pool_out[l], w_conv_out[l], w_o[l])

        h = h + 0.5 * m[:, 8] * swiglu(
            modulate(rms_norm(h, norm_g[l, 2]), m[:, 6], m[:, 7]), ffn2_wi[l], ffn2_wo[l])
        if not last:
            hc = hc + 0.5 * mc[8] * swiglu(
                modulate(rms_norm(hc, norm_g[l, 2]), mc[6], mc[7]), ffn2_wi[l], ffn2_wo[l])
    return rms_norm(h, final_g)
```

```python
import functools

import jax
import jax.numpy as jnp
from jax import lax
from jax.experimental import pallas as pl
from jax.experimental.pallas import tpu as pltpu

D_MODEL = 2048
SEQ = 8192
CTX_LEN = 256
ROWS = SEQ + CTX_LEN
DEPTH = 2
GRID_W = 64
HEAD_DIM = 128
N_Q_HEADS = 16
N_KV_HEADS = 4
GQA_GROUP = 4
WINDOW = 128
BLOCK = 128
ROPE_PAIRS = 32
ROPE_BASE = 10000.0
POOL_WINDOWS = (2, 4, 8, 16)
POOL_WIDTH = 1024
POOL_GROUP = 256
CONV_WIDTH = 1024
D_FF = 5632
N_MODS = 9
EPS = 1e-6
NEG_INF = -1e30

K_OFF = 2048
V_OFF = 2560
POOL_OFF = 3072
CB_OFF = 4096
CC_OFF = 5120
CX_OFF = 6144
GATE_OFF = 7168
IN_COLS = GATE_OFF + 3 * D_MODEL
GATE_STORE_OFF = 8192
P_COLS = GATE_STORE_OFF + 3 * D_MODEL

BF16 = jnp.bfloat16
F32 = jnp.float32

VMEM_LIMIT_BYTES = 58 * 1024 * 1024

FFN_TM = 768
FFN_TM_LATENT = 1024
FFN_TF = 512
FFN_TF_LATENT = 256
PROJ_TN = 1024
ROW_CHUNK = 128
COL_CHUNK = 512
MIX_TM = 256
HALO = 16
POOL_K = 384


def _cparams(sem):
    return pltpu.CompilerParams(dimension_semantics=sem, vmem_limit_bytes=VMEM_LIMIT_BYTES)


def _dot(a, b):
    return jnp.dot(a, b, preferred_element_type=F32)


def _norm_mod(x, g, shift, scale):
    y = x * lax.rsqrt(jnp.mean(x * x, axis=-1, keepdims=True) + EPS)
    return (y * g) * (1.0 + scale) + shift


def _mod_rows(m_ref, k, is_ctx):
    lo, hi = k * D_MODEL, (k + 1) * D_MODEL
    return jnp.where(is_ctx, m_ref[1:2, lo:hi], m_ref[0:1, lo:hi])


def _is_ctx_rows(first_row):
    return (first_row + lax.broadcasted_iota(jnp.int32, (ROW_CHUNK, 1), 0)) >= SEQ


def _rows_loop(tm, body):
    def step(r, carry):
        body(pl.multiple_of(r * ROW_CHUNK, ROW_CHUNK))
        return carry
    lax.fori_loop(0, tm // ROW_CHUNK, step, 0)


def _norm_mod_rows(x_ref, xm_ref, g_ref, m_ref, k_shift, row0, tm):
    def rows(r0):
        rs = pl.ds(r0, ROW_CHUNK)
        is_ctx = _is_ctx_rows(row0 + r0)
        xm_ref[rs, :] = _norm_mod(x_ref[rs, :], g_ref[...], _mod_rows(m_ref, k_shift, is_ctx),
                                  _mod_rows(m_ref, k_shift + 1, is_ctx)).astype(BF16)
    _rows_loop(tm, rows)


def _ada_kernel(c_ref, w_ref, b_ref, o_ref):
    c = c_ref[...]
    s = c * jax.nn.sigmoid(c)
    o_ref[...] = _dot(s.astype(BF16), w_ref[...].astype(BF16)) + b_ref[...]


def _ada(cond, w_ada, b_ada):
    tn = 1024
    n_cols = N_MODS * D_MODEL
    return pl.pallas_call(
        _ada_kernel,
        grid=(DEPTH, n_cols // tn),
        in_specs=[
            pl.BlockSpec((8, D_MODEL), lambda l, j: (0, 0)),
            pl.BlockSpec((None, D_MODEL, tn), lambda l, j: (l, 0, j)),
            pl.BlockSpec((None, 1, tn), lambda l, j: (l, 0, j)),
        ],
        out_specs=pl.BlockSpec((None, 8, tn), lambda l, j: (l, 0, j)),
        out_shape=jax.ShapeDtypeStruct((DEPTH, 8, n_cols), F32),
        compiler_params=_cparams(("arbitrary", "arbitrary")),
        name="ada",
    )(cond, w_ada, b_ada.reshape(DEPTH, 1, n_cols))


def _ffn_kernel(x_ref, m_ref, g_ref, wig_ref, wiu_ref, wo_ref, fg_ref, o_ref, xm_ref, acc_ref,
                *, tm, k0, final):
    row0 = pl.program_id(0) * tm
    f = pl.program_id(1)

    @pl.when(f == 0)
    def _():
        _norm_mod_rows(x_ref, xm_ref, g_ref, m_ref, k0, row0, tm)
        acc_ref[...] = jnp.zeros_like(acc_ref)

    xm = xm_ref[...]
    g = _dot(xm, wig_ref[...])
    u = _dot(xm, wiu_ref[...])
    a = ((g * jax.nn.sigmoid(g)) * u).astype(BF16)
    for c0 in range(0, D_MODEL, COL_CHUNK):
        acc_ref[:, c0:c0 + COL_CHUNK] += _dot(a, wo_ref[:, c0:c0 + COL_CHUNK])

    @pl.when(f == pl.num_programs(1) - 1)
    def _():
        def rows(r0):
            rs = pl.ds(r0, ROW_CHUNK)
            gate = _mod_rows(m_ref, k0 + 2, _is_ctx_rows(row0 + r0))
            y = x_ref[rs, :] + (0.5 * gate) * acc_ref[rs, :]
            if final:
                y = (y * lax.rsqrt(jnp.mean(y * y, axis=-1, keepdims=True) + EPS)) * fg_ref[...]
            o_ref[rs, :] = y
        _rows_loop(tm, rows)


def _ffn(x, n_rows, tm, tf, mods, norm_g, wi, wo, layer, k0, final_g=None):
    final = final_g is not None
    if final_g is None:
        final_g = norm_g
    n_f = D_FF // tf
    kern = functools.partial(_ffn_kernel, tm=tm, k0=k0, final=final)
    return pl.pallas_call(
        kern,
        grid=(n_rows // tm, n_f),
        in_specs=[
            pl.BlockSpec((tm, D_MODEL), lambda i, f: (i, 0)),
            pl.BlockSpec((8, N_MODS * D_MODEL), lambda i, f: (0, 0)),
            pl.BlockSpec((1, D_MODEL), lambda i, f: (0, 0)),
            pl.BlockSpec((None, D_MODEL, tf), lambda i, f: (layer, 0, f)),
            pl.BlockSpec((None, D_MODEL, tf), lambda i, f: (layer, 0, n_f + f)),
            pl.BlockSpec((None, tf, D_MODEL), lambda i, f: (layer, f, 0)),
            pl.BlockSpec((1, D_MODEL), lambda i, f: (0, 0)),
        ],
        out_specs=pl.BlockSpec((tm, D_MODEL), lambda i, f: (i, 0)),
        out_shape=jax.ShapeDtypeStruct((n_rows, D_MODEL), F32),
        scratch_shapes=[pltpu.VMEM((tm, D_MODEL), BF16), pltpu.VMEM((tm, D_MODEL), F32)],
        compiler_params=_cparams(("arbitrary", "arbitrary")),
        name="ffn_final" if final else "ffn",
    )(x, mods, norm_g.reshape(1, D_MODEL), wi, wi, wo, final_g.reshape(1, D_MODEL))


def _rope(t, cos, sin):
    lane = lax.broadcasted_iota(jnp.int32, (1, HEAD_DIM), 1)
    first_half = (lane % (2 * ROPE_PAIRS)) < ROPE_PAIRS
    outs = []
    for hd in range(t.shape[1] // HEAD_DIM):
        th = t[:, hd * HEAD_DIM:(hd + 1) * HEAD_DIM]
        partner = jnp.where(first_half, pltpu.roll(th, HEAD_DIM - ROPE_PAIRS, 1),
                            pltpu.roll(th, ROPE_PAIRS, 1))
        outs.append(th * cos + partner * sin)
    return jnp.concatenate(outs, axis=1)


def _inproj_kernel(x_ref, m_ref, g_ref, w_ref, cos_ref, sin_ref, o_ref, xn_ref, *, tm):
    j = pl.program_id(1)

    @pl.when(j == 0)
    def _():
        _norm_mod_rows(x_ref, xn_ref, g_ref, m_ref, 3, pl.program_id(0) * tm, tm)

    acc = _dot(xn_ref[...], w_ref[...])
    q_blocks = K_OFF // PROJ_TN
    plain_end = GATE_OFF // PROJ_TN

    @pl.when(j < q_blocks)
    def _():
        o_ref[...] = (_rope(acc, cos_ref[...], sin_ref[...]) * (HEAD_DIM ** -0.5)).astype(BF16)

    @pl.when(j == q_blocks)
    def _():
        kw = V_OFF - K_OFF
        o_ref[:, :kw] = _rope(acc[:, :kw], cos_ref[...], sin_ref[...]).astype(BF16)
        o_ref[:, kw:] = acc[:, kw:].astype(BF16)

    @pl.when((j > q_blocks) & (j < plain_end))
    def _():
        o_ref[...] = acc.astype(BF16)

    @pl.when(j >= plain_end)
    def _():
        o_ref[...] = jax.nn.sigmoid(acc).astype(BF16)


def _inproj(x, mods, norm_g, w_in, layer, cos, sin):
    tm = FFN_TM
    plain_end = GATE_OFF // PROJ_TN
    shift = (GATE_STORE_OFF - GATE_OFF) // PROJ_TN
    return pl.pallas_call(
        functools.partial(_inproj_kernel, tm=tm),
        grid=(ROWS // tm, IN_COLS // PROJ_TN),
        in_specs=[
            pl.BlockSpec((tm, D_MODEL), lambda i, j: (i, 0)),
            pl.BlockSpec((8, N_MODS * D_MODEL), lambda i, j: (0, 0)),
            pl.BlockSpec((1, D_MODEL), lambda i, j: (0, 0)),
            pl.BlockSpec((None, D_MODEL, PROJ_TN), lambda i, j: (layer, 0, j)),
            pl.BlockSpec((tm, HEAD_DIM), lambda i, j: (i, 0)),
            pl.BlockSpec((tm, HEAD_DIM), lambda i, j: (i, 0)),
        ],
        out_specs=pl.BlockSpec((tm, PROJ_TN), lambda i, j: (i, jnp.where(j >= plain_end, j + shift, j))),
        out_shape=jax.ShapeDtypeStruct((ROWS, P_COLS), BF16),
        scratch_shapes=[pltpu.VMEM((tm, D_MODEL), BF16)],
        compiler_params=_cparams(("arbitrary", "arbitrary")),
        name="inproj",
    )(x, mods, norm_g.reshape(1, D_MODEL), w_in, cos, sin)


def _rope_tables():
    pos = jnp.arange(SEQ, dtype=jnp.int32)
    r = (pos // GRID_W).astype(F32)
    col = (pos % GRID_W).astype(F32)
    inv = ROPE_BASE ** (-jnp.arange(ROPE_PAIRS, dtype=F32) / ROPE_PAIRS)
    ang_r, ang_c = r[:, None] * inv, col[:, None] * inv
    cos = jnp.concatenate([jnp.cos(ang_r), jnp.cos(ang_r), jnp.cos(ang_c), jnp.cos(ang_c)], axis=1)
    sin = jnp.concatenate([-jnp.sin(ang_r), jnp.sin(ang_r), -jnp.sin(ang_c), jnp.sin(ang_c)], axis=1)
    cos = jnp.concatenate([cos, jnp.ones((CTX_LEN, HEAD_DIM), F32)], axis=0)
    sin = jnp.concatenate([sin, jnp.zeros((CTX_LEN, HEAD_DIM), F32)], axis=0)
    return cos, sin


def _attn_kernel(sink_ref, q_ref, kp_ref, kc_ref, kn_ref, vp_ref, vc_ref, vn_ref, kx_ref, vx_ref, o_ref):
    n = pl.program_id(0)
    n_lat = SEQ // BLOCK
    r = lax.broadcasted_iota(jnp.int32, (BLOCK, 3 * BLOCK), 0)
    c = lax.broadcasted_iota(jnp.int32, (BLOCK, 3 * BLOCK), 1)
    rel = c - BLOCK - r
    valid = (jnp.abs(rel) <= WINDOW) & ((c >= BLOCK) | (n >= 1)) & ((c < 2 * BLOCK) | (n < n_lat - 1))
    valid = valid & (n < n_lat)
    valid = jnp.concatenate([valid] * GQA_GROUP, axis=0)
    contract_last = (((1,), (1,)), ((), ()))
    for h in range(N_KV_HEADS):
        hs = slice(h * HEAD_DIM, (h + 1) * HEAD_DIM)
        qg = jnp.concatenate(
            [q_ref[:, (h * GQA_GROUP + g) * HEAD_DIM:(h * GQA_GROUP + g + 1) * HEAD_DIM]
             for g in range(GQA_GROUP)], axis=0)
        kb = jnp.concatenate([kp_ref[:, hs], kc_ref[:, hs], kn_ref[:, hs]], axis=0)
        vb = jnp.concatenate([vp_ref[:, hs], vc_ref[:, hs], vn_ref[:, hs]], axis=0)
        s_loc = lax.dot_general(qg, kb, contract_last, preferred_element_type=F32)
        s_loc = jnp.where(valid, s_loc, NEG_INF)
        s_ctx = lax.dot_general(qg, kx_ref[:, hs], contract_last, preferred_element_type=F32)
        s_sink = jnp.concatenate(
            [jnp.full((BLOCK, 1), sink_ref[h * GQA_GROUP + g], F32) for g in range(GQA_GROUP)], axis=0)
        m = jnp.maximum(jnp.maximum(jnp.max(s_loc, axis=-1, keepdims=True),
                                    jnp.max(s_ctx, axis=-1, keepdims=True)), s_sink)
        p_loc = jnp.exp(s_loc - m)
        p_ctx = jnp.exp(s_ctx - m)
        denom = (jnp.sum(p_loc, axis=-1, keepdims=True) + jnp.sum(p_ctx, axis=-1, keepdims=True)
                 + jnp.exp(s_sink - m))
        o = (_dot(p_loc.astype(BF16), vb) + _dot(p_ctx.astype(BF16), vx_ref[:, hs])) / denom
        for g in range(GQA_GROUP):
            hq = h * GQA_GROUP + g
            o_ref[:, hq * HEAD_DIM:(hq + 1) * HEAD_DIM] = o[g * BLOCK:(g + 1) * BLOCK].astype(BF16)


def _attention(p, sink, n_rows):
    n_blocks = n_rows // BLOCK
    last = ROWS // BLOCK - 1
    kcol, vcol = K_OFF // 512, V_OFF // 512
    ctx_blk = SEQ // CTX_LEN
    kv = lambda col, off: pl.BlockSpec(
        (BLOCK, 512), lambda n, s: (jnp.clip(n + off, 0, last), col))
    return pl.pallas_call(
        _attn_kernel,
        grid_spec=pltpu.PrefetchScalarGridSpec(
            num_scalar_prefetch=1,
            grid=(n_blocks,),
            in_specs=[
                pl.BlockSpec((BLOCK, D_MODEL), lambda n, s: (n, 0)),
                kv(kcol, -1), kv(kcol, 0), kv(kcol, 1),
                kv(vcol, -1), kv(vcol, 0), kv(vcol, 1),
                pl.BlockSpec((CTX_LEN, 512), lambda n, s: (ctx_blk, kcol)),
                pl.BlockSpec((CTX_LEN, 512), lambda n, s: (ctx_blk, vcol)),
            ],
            out_specs=pl.BlockSpec((BLOCK, D_MODEL), lambda n, s: (n, 0)),
        ),
        out_shape=jax.ShapeDtypeStruct((n_rows, D_MODEL), BF16),
        compiler_params=_cparams(("arbitrary",)),
        name="attention",
    )(sink, p, p, p, p, p, p, p, p, p)


def _mix_kernel(x_ref, a_ref, u_ref, up_ref, un_ref, cb_ref, cc_ref, ccp_ref, ccn_ref,
                cx_ref, cxp_ref, cxn_ref, g0_ref, g1_ref, g2_ref, m_ref, pw_ref, ps_ref, cw_ref,
                wa_ref, wp_ref, wc_ref, wo_ref, o_ref, ub_ref):
    t = pl.program_id(0)
    tm = MIX_TM
    lat_tiles = SEQ // tm
    is_ctx = t >= lat_tiles
    seg_first = (t == 0) | is_ctx
    seg_last = (t == lat_tiles - 1) | is_ctx
    seg_len = jnp.where(is_ctx, CTX_LEN, SEQ)
    row = lax.broadcasted_iota(jnp.int32, (tm, 1), 0)
    pos = jnp.where(is_ctx, 0, t * tm) + row

    y_attn = _dot(a_ref[...], wa_ref[...])

    ub_ref[0:tm, :] = u_ref[...]
    ub_ref[tm:tm + HALO, :] = jnp.where(seg_last, jnp.zeros_like(un_ref[...]), un_ref[...])
    ub_ref[tm + HALO:POOL_K - HALO, :] = jnp.zeros((POOL_K - 2 * HALO - tm, POOL_WIDTH), BF16)
    ub_ref[POOL_K - HALO:POOL_K, :] = jnp.where(seg_first, jnp.zeros_like(up_ref[...]), up_ref[...])
    br = lax.broadcasted_iota(jnp.int32, (tm, POOL_K), 0)
    bc = lax.broadcasted_iota(jnp.int32, (tm, POOL_K), 1)
    d = bc - br
    d = jnp.where(d >= POOL_K // 2, d - POOL_K, d)
    pooled = []
    for gi, w in enumerate(POOL_WINDOWS):
        lo, hi = w // 2, w - w // 2 - 1
        gs = slice(gi * POOL_GROUP, (gi + 1) * POOL_GROUP)
        band = ((d >= -lo) & (d <= hi)).astype(BF16)
        wsum = _dot(band, ub_ref[:, gs])
        cnt = (jnp.minimum(pos + hi, seg_len - 1) - jnp.maximum(pos - lo, 0) + 1).astype(F32)
        y = wsum / cnt - u_ref[:, gs].astype(F32)
        pooled.append(_dot(y.astype(BF16), pw_ref[gi]))
    y_pool = jnp.concatenate(pooled, axis=1) * ps_ref[...]
    y_pool = _dot(y_pool.astype(BF16), wp_ref[...])

    z = cc_ref[...].astype(F32) * cx_ref[...].astype(F32)
    z_prev = ccp_ref[HALO - 1:HALO, :].astype(F32) * cxp_ref[HALO - 1:HALO, :].astype(F32)
    z_next = ccn_ref[0:1, :].astype(F32) * cxn_ref[0:1, :].astype(F32)
    z_prev = jnp.where(seg_first, 0.0, z_prev)
    z_next = jnp.where(seg_last, 0.0, z_next)
    zm1 = jnp.where(row == 0, z_prev, pltpu.roll(z, 1, 0))
    zp1 = jnp.where(row == tm - 1, z_next, pltpu.roll(z, tm - 1, 0))
    conv = zm1 * cw_ref[0:1, :] + z * cw_ref[1:2, :] + zp1 * cw_ref[2:3, :]
    y_conv = _dot((cb_ref[...].astype(F32) * conv).astype(BF16), wc_ref[...])

    merged = (g0_ref[...].astype(F32) * y_attn + g1_ref[...].astype(F32) * y_pool
              + g2_ref[...].astype(F32) * y_conv)
    out = _dot(merged.astype(BF16), wo_ref[...])
    o_ref[...] = x_ref[...] + _mod_rows(m_ref, 5, is_ctx) * out


def _mix(x, p, attn, mods, pool_w, pool_scale, conv_w, w_attn_out, w_pool_out, w_conv_out, w_o, n_rows):
    tm = MIX_TM
    n_tiles = n_rows // tm
    hb = tm // HALO
    last_halo = ROWS // HALO - 1
    tile = lambda width, col: pl.BlockSpec((tm, width), lambda t: (t, col))
    prev = lambda col: pl.BlockSpec((HALO, POOL_WIDTH), lambda t: (jnp.maximum(t * hb - 1, 0), col))
    nxt = lambda col: pl.BlockSpec((HALO, POOL_WIDTH), lambda t: (jnp.minimum((t + 1) * hb, last_halo), col))
    const = lambda shape: pl.BlockSpec(shape, lambda t: (0,) * len(shape), pipeline_mode=pl.Buffered(1))
    ucol, cbcol, cccol, cxcol = POOL_OFF // 1024, CB_OFF // 1024, CC_OFF // 1024, CX_OFF // 1024
    gcol = GATE_STORE_OFF // D_MODEL
    return pl.pallas_call(
        _mix_kernel,
        grid=(n_tiles,),
        in_specs=[
            tile(D_MODEL, 0),
            tile(D_MODEL, 0),
            tile(1024, ucol), prev(ucol), nxt(ucol),
            tile(1024, cbcol),
            tile(1024, cccol), prev(cccol), nxt(cccol),
            tile(1024, cxcol), prev(cxcol), nxt(cxcol),
            tile(D_MODEL, gcol), tile(D_MODEL, gcol + 1), tile(D_MODEL, gcol + 2),
            const((8, N_MODS * D_MODEL)),
            const((len(POOL_WINDOWS), POOL_GROUP, POOL_GROUP)),
            const((1, POOL_WIDTH)),
            const((3, CONV_WIDTH)),
            const((D_MODEL, D_MODEL)),
            const((POOL_WIDTH, D_MODEL)),
            const((CONV_WIDTH, D_MODEL)),
            const((D_MODEL, D_MODEL)),
        ],
        out_specs=pl.BlockSpec((tm, D_MODEL), lambda t: (t, 0)),
        out_shape=jax.ShapeDtypeStruct((n_rows, D_MODEL), F32),
        scratch_shapes=[pltpu.VMEM((POOL_K, POOL_WIDTH), BF16)],
        compiler_params=_cparams(("arbitrary",)),
        name="mix",
    )(x, attn, p, p, p, p, p, p, p, p, p, p, p, p, p, mods, pool_w, pool_scale.reshape(1, POOL_WIDTH),
      conv_w, w_attn_out, w_pool_out, w_conv_out, w_o)


def kernel(x, c, ctx, c_ctx, w_ada, b_ada, norm_g, ffn1_wi, ffn1_wo, w_in, attn_sink, pool_w, pool_scale,
           conv_w, w_attn_out, w_pool_out, w_conv_out, w_o, ffn2_wi, ffn2_wo, final_g):
    assert x.shape == (1, SEQ, D_MODEL) and ctx.shape == (1, CTX_LEN, D_MODEL)
    cond = jnp.concatenate([c.reshape(1, D_MODEL), c_ctx.reshape(1, D_MODEL),
                            jnp.zeros((6, D_MODEL), F32)], axis=0)
    mods = _ada(cond, w_ada, b_ada)
    cos, sin = _rope_tables()
    bf = lambda w: w.astype(BF16)
    ffn1_wi, ffn1_wo, ffn2_wi, ffn2_wo = bf(ffn1_wi), bf(ffn1_wo), bf(ffn2_wi), bf(ffn2_wo)
    w_in, pool_w = bf(w_in), bf(pool_w)
    w_attn_out, w_pool_out, w_conv_out, w_o = bf(w_attn_out), bf(w_pool_out), bf(w_conv_out), bf(w_o)

    h = jnp.concatenate([x[0], ctx[0]], axis=0)
    for l in range(DEPTH):
        last = l == DEPTH - 1
        m = mods[l]
        h = _ffn(h, ROWS, FFN_TM, FFN_TF, m, norm_g[l, 0], ffn1_wi, ffn1_wo, l, 0)
        p = _inproj(h, m, norm_g[l, 1], w_in, l, cos, sin)
        n_rows = SEQ if last else ROWS
        attn = _attention(p, attn_sink[l], n_rows)
        h = _mix(h, p, attn, m, pool_w[l], pool_scale[l], conv_w[l], w_attn_out[l], w_pool_out[l],
                 w_conv_out[l], w_o[l], n_rows)
        if last:
            h = _ffn(h, SEQ, FFN_TM_LATENT, FFN_TF_LATENT, m, norm_g[l, 2], ffn2_wi, ffn2_wo, l, 6,
                     final_g=final_g)
        else:
            h = _ffn(h, ROWS, FFN_TM, FFN_TF, m, norm_g[l, 2], ffn2_wi, ffn2_wo, l, 6)
    return h.reshape(1, SEQ, D_MODEL)
```

```python
import functools

import jax
import jax.numpy as jnp
from jax import lax
from jax.experimental import pallas as pl
from jax.experimental.pallas import tpu as pltpu

D_MODEL = 2048
SEQ = 8192
CTX_LEN = 256
ROWS = SEQ + CTX_LEN
DEPTH = 2
GRID_W = 64
HEAD_DIM = 128
N_Q_HEADS = 16
N_KV_HEADS = 4
GQA_GROUP = 4
WINDOW = 128
BLOCK = 128
ROPE_PAIRS = 32
ROPE_BASE = 10000.0
POOL_WINDOWS = (2, 4, 8, 16)
POOL_WIDTH = 1024
POOL_GROUP = 256
CONV_WIDTH = 1024
D_FF = 5632
N_MODS = 9
EPS = 1e-6
NEG_INF = -1e30
LOG2E = 1.4426950408889634

K_OFF = 2048
V_OFF = 2560
POOL_OFF = 3072
CB_OFF = 4096
CC_OFF = 5120
CX_OFF = 6144
GATE_OFF = 7168
IN_COLS = GATE_OFF + 3 * D_MODEL
GATE_STORE_OFF = 8192
P_COLS = GATE_STORE_OFF + 3 * D_MODEL

BF16 = jnp.bfloat16
F32 = jnp.float32

VMEM_LIMIT_BYTES = 58 * 1024 * 1024

FFN_TM = 768
FFN_TM_LATENT = 1024
FFN_TF = 512
FFN_TF_LATENT = 256
PROJ_TN = 1024
PROJ_CHUNK = 512
ROW_CHUNK = 128
COL_CHUNK = 512
MIX_TM = 256
HALO = 16
POOL_K = 384


def _cparams(sem):
    return pltpu.CompilerParams(dimension_semantics=sem, vmem_limit_bytes=VMEM_LIMIT_BYTES)


def _dot(a, b):
    return jnp.dot(a, b, preferred_element_type=F32)


def _norm_mod(x, g, shift, scale):
    y = x * lax.rsqrt(jnp.mean(x * x, axis=-1, keepdims=True) + EPS)
    return (y * g) * (1.0 + scale) + shift


def _mod_rows(m_ref, k, is_ctx):
    lo, hi = k * D_MODEL, (k + 1) * D_MODEL
    return jnp.where(is_ctx, m_ref[1:2, lo:hi], m_ref[0:1, lo:hi])


def _is_ctx_rows(first_row):
    return (first_row + lax.broadcasted_iota(jnp.int32, (ROW_CHUNK, 1), 0)) >= SEQ


def _rows_loop(tm, body):
    def step(r, carry):
        body(pl.multiple_of(r * ROW_CHUNK, ROW_CHUNK))
        return carry
    lax.fori_loop(0, tm // ROW_CHUNK, step, 0)


def _norm_mod_rows(x_ref, xm_ref, g_ref, m_ref, k_shift, row0, tm):
    def rows(r0):
        rs = pl.ds(r0, ROW_CHUNK)
        is_ctx = _is_ctx_rows(row0 + r0)
        xm_ref[rs, :] = _norm_mod(x_ref[rs, :], g_ref[...], _mod_rows(m_ref, k_shift, is_ctx),
                                  _mod_rows(m_ref, k_shift + 1, is_ctx)).astype(BF16)
    _rows_loop(tm, rows)


def _ada_kernel(c_ref, w_ref, b_ref, o_ref):
    c = c_ref[...]
    s = c * jax.nn.sigmoid(c)
    o_ref[...] = _dot(s.astype(BF16), w_ref[...].astype(BF16)) + b_ref[...]


def _ada(cond, w_ada, b_ada):
    tn = 1024
    n_cols = N_MODS * D_MODEL
    return pl.pallas_call(
        _ada_kernel,
        grid=(DEPTH, n_cols // tn),
        in_specs=[
            pl.BlockSpec((8, D_MODEL), lambda l, j: (0, 0)),
            pl.BlockSpec((None, D_MODEL, tn), lambda l, j: (l, 0, j)),
            pl.BlockSpec((None, 1, tn), lambda l, j: (l, 0, j)),
        ],
        out_specs=pl.BlockSpec((None, 8, tn), lambda l, j: (l, 0, j)),
        out_shape=jax.ShapeDtypeStruct((DEPTH, 8, n_cols), F32),
        compiler_params=_cparams(("arbitrary", "arbitrary")),
        name="ada",
    )(cond, w_ada, b_ada.reshape(DEPTH, 1, n_cols))


def _ffn_kernel(x_ref, m_ref, g_ref, wig_ref, wiu_ref, wo_ref, fg_ref, o_ref, xm_ref, acc_ref,
                *, tm, k0, final):
    row0 = pl.program_id(0) * tm
    f = pl.program_id(1)

    @pl.when(f == 0)
    def _():
        _norm_mod_rows(x_ref, xm_ref, g_ref, m_ref, k0, row0, tm)
        acc_ref[...] = jnp.zeros_like(acc_ref)

    xm = xm_ref[...]
    g = _dot(xm, wig_ref[...])
    u = _dot(xm, wiu_ref[...])
    a = ((g * jax.nn.sigmoid(g)) * u).astype(BF16)
    for c0 in range(0, D_MODEL, COL_CHUNK):
        acc_ref[:, c0:c0 + COL_CHUNK] += _dot(a, wo_ref[:, c0:c0 + COL_CHUNK])

    @pl.when(f == pl.num_programs(1) - 1)
    def _():
        def rows(r0):
            rs = pl.ds(r0, ROW_CHUNK)
            gate = _mod_rows(m_ref, k0 + 2, _is_ctx_rows(row0 + r0))
            y = x_ref[rs, :] + (0.5 * gate) * acc_ref[rs, :]
            if final:
                y = (y * lax.rsqrt(jnp.mean(y * y, axis=-1, keepdims=True) + EPS)) * fg_ref[...]
            o_ref[rs, :] = y
        _rows_loop(tm, rows)


def _ffn(x, n_rows, tm, tf, mods, norm_g, wi, wo, layer, k0, final_g=None):
    final = final_g is not None
    if final_g is None:
        final_g = norm_g
    n_f = D_FF // tf
    kern = functools.partial(_ffn_kernel, tm=tm, k0=k0, final=final)
    return pl.pallas_call(
        kern,
        grid=(n_rows // tm, n_f),
        in_specs=[
            pl.BlockSpec((tm, D_MODEL), lambda i, f: (i, 0)),
            pl.BlockSpec((8, N_MODS * D_MODEL), lambda i, f: (0, 0)),
            pl.BlockSpec((1, D_MODEL), lambda i, f: (0, 0)),
            pl.BlockSpec((None, D_MODEL, tf), lambda i, f: (layer, 0, f)),
            pl.BlockSpec((None, D_MODEL, tf), lambda i, f: (layer, 0, n_f + f)),
            pl.BlockSpec((None, tf, D_MODEL), lambda i, f: (layer, f, 0)),
            pl.BlockSpec((1, D_MODEL), lambda i, f: (0, 0)),
        ],
        out_specs=pl.BlockSpec((tm, D_MODEL), lambda i, f: (i, 0)),
        out_shape=jax.ShapeDtypeStruct((n_rows, D_MODEL), F32),
        scratch_shapes=[pltpu.VMEM((tm, D_MODEL), BF16), pltpu.VMEM((tm, D_MODEL), F32)],
        compiler_params=_cparams(("arbitrary", "arbitrary")),
        name="ffn_final" if final else "ffn",
    )(x, mods, norm_g.reshape(1, D_MODEL), wi, wi, wo, final_g.reshape(1, D_MODEL))


def _rope(t, cos, sin):
    lane = lax.broadcasted_iota(jnp.int32, (1, HEAD_DIM), 1)
    first_half = (lane % (2 * ROPE_PAIRS)) < ROPE_PAIRS
    outs = []
    for hd in range(t.shape[1] // HEAD_DIM):
        th = t[:, hd * HEAD_DIM:(hd + 1) * HEAD_DIM]
        partner = jnp.where(first_half, pltpu.roll(th, HEAD_DIM - ROPE_PAIRS, 1),
                            pltpu.roll(th, ROPE_PAIRS, 1))
        outs.append(th * cos + partner * sin)
    return jnp.concatenate(outs, axis=1)


def _inproj_kernel(x_ref, m_ref, g_ref, w_ref, cos_ref, sin_ref, o_ref, xn_ref, *, tm):
    j = pl.program_id(1)

    @pl.when(j == 0)
    def _():
        _norm_mod_rows(x_ref, xn_ref, g_ref, m_ref, 3, pl.program_id(0) * tm, tm)

    q_blocks = K_OFF // PROJ_TN
    plain_end = GATE_OFF // PROJ_TN

    def project(epilogue):
        for c0 in range(0, PROJ_TN, PROJ_CHUNK):
            cs = slice(c0, c0 + PROJ_CHUNK)
            o_ref[:, cs] = epilogue(_dot(xn_ref[...], w_ref[:, cs]), c0).astype(BF16)

    def rope(acc):
        return _rope(acc, cos_ref[...], sin_ref[...])

    @pl.when(j < q_blocks)
    def _():
        project(lambda acc, c0: rope(acc) * (HEAD_DIM ** -0.5 * LOG2E))

    @pl.when(j == q_blocks)
    def _():
        project(lambda acc, c0: rope(acc) if c0 < V_OFF - K_OFF else acc)

    @pl.when((j > q_blocks) & (j < plain_end))
    def _():
        project(lambda acc, c0: acc)

    @pl.when(j >= plain_end)
    def _():
        project(lambda acc, c0: jax.nn.sigmoid(acc))


def _inproj(x, mods, norm_g, w_in, layer, cos, sin):
    tm = FFN_TM
    plain_end = GATE_OFF // PROJ_TN
    shift = (GATE_STORE_OFF - GATE_OFF) // PROJ_TN
    return pl.pallas_call(
        functools.partial(_inproj_kernel, tm=tm),
        grid=(ROWS // tm, IN_COLS // PROJ_TN),
        in_specs=[
            pl.BlockSpec((tm, D_MODEL), lambda i, j: (i, 0)),
            pl.BlockSpec((8, N_MODS * D_MODEL), lambda i, j: (0, 0)),
            pl.BlockSpec((1, D_MODEL), lambda i, j: (0, 0)),
            pl.BlockSpec((None, D_MODEL, PROJ_TN), lambda i, j: (layer, 0, j)),
            pl.BlockSpec((tm, HEAD_DIM), lambda i, j: (i, 0)),
            pl.BlockSpec((tm, HEAD_DIM), lambda i, j: (i, 0)),
        ],
        out_specs=pl.BlockSpec((tm, PROJ_TN), lambda i, j: (i, jnp.where(j >= plain_end, j + shift, j))),
        out_shape=jax.ShapeDtypeStruct((ROWS, P_COLS), BF16),
        scratch_shapes=[pltpu.VMEM((tm, D_MODEL), BF16)],
        compiler_params=_cparams(("arbitrary", "arbitrary")),
        name="inproj",
    )(x, mods, norm_g.reshape(1, D_MODEL), w_in, cos, sin)


def _rope_tables():
    pos = jnp.arange(SEQ, dtype=jnp.int32)
    r = (pos // GRID_W).astype(F32)
    col = (pos % GRID_W).astype(F32)
    inv = ROPE_BASE ** (-jnp.arange(ROPE_PAIRS, dtype=F32) / ROPE_PAIRS)
    ang_r, ang_c = r[:, None] * inv, col[:, None] * inv
    cos = jnp.concatenate([jnp.cos(ang_r), jnp.cos(ang_r), jnp.cos(ang_c), jnp.cos(ang_c)], axis=1)
    sin = jnp.concatenate([-jnp.sin(ang_r), jnp.sin(ang_r), -jnp.sin(ang_c), jnp.sin(ang_c)], axis=1)
    cos = jnp.concatenate([cos, jnp.ones((CTX_LEN, HEAD_DIM), F32)], axis=0)
    sin = jnp.concatenate([sin, jnp.zeros((CTX_LEN, HEAD_DIM), F32)], axis=0)
    return cos, sin


def _attn_kernel(sink_ref, q_ref, kp_ref, kc_ref, kn_ref, vp_ref, vc_ref, vn_ref, kx_ref, vx_ref, o_ref):
    n = pl.program_id(0)
    n_lat = SEQ // BLOCK
    r = lax.broadcasted_iota(jnp.int32, (BLOCK, BLOCK), 0)
    c = lax.broadcasted_iota(jnp.int32, (BLOCK, BLOCK), 1)
    is_lat = n < n_lat
    prev_ok = (c >= r) & (n >= 1) & is_lat
    next_ok = (c <= r) & (n < n_lat - 1)
    n_keys = 3 * BLOCK + CTX_LEN
    ones = jnp.ones((n_keys, HEAD_DIM), BF16)
    contract_last = (((1,), (1,)), ((), ()))
    for h in range(N_KV_HEADS):
        hs = slice(h * HEAD_DIM, (h + 1) * HEAD_DIM)
        qg = jnp.concatenate(
            [q_ref[:, (h * GQA_GROUP + g) * HEAD_DIM:(h * GQA_GROUP + g + 1) * HEAD_DIM]
             for g in range(GQA_GROUP)], axis=0)
        kb = jnp.concatenate([kp_ref[:, hs], kc_ref[:, hs], kn_ref[:, hs], kx_ref[:, hs]], axis=0)
        vb = jnp.concatenate([vp_ref[:, hs], vc_ref[:, hs], vn_ref[:, hs], vx_ref[:, hs]], axis=0)
        vb = jnp.concatenate([vb, ones], axis=1)
        s = lax.dot_general(qg, kb, contract_last, preferred_element_type=F32)
        probs, maxes, sinks = [], [], []
        for g in range(GQA_GROUP):
            sg = s[g * BLOCK:(g + 1) * BLOCK]
            parts = [jnp.where(prev_ok, sg[:, 0:BLOCK], NEG_INF),
                     jnp.where(is_lat, sg[:, BLOCK:2 * BLOCK], NEG_INF),
                     jnp.where(next_ok, sg[:, 2 * BLOCK:3 * BLOCK], NEG_INF),
                     sg[:, 3 * BLOCK:4 * BLOCK], sg[:, 4 * BLOCK:5 * BLOCK]]
            sink = sink_ref[h * GQA_GROUP + g] * LOG2E
            lane_max = jnp.maximum(jnp.maximum(jnp.maximum(parts[0], parts[1]),
                                               jnp.maximum(parts[2], parts[3])), parts[4])
            m = jnp.maximum(jnp.max(lane_max, axis=-1, keepdims=True), sink)
            probs.append(jnp.concatenate([jnp.exp2(pt - m).astype(BF16) for pt in parts], axis=1))
            maxes.append(m)
            sinks.append(sink)
        oe = _dot(jnp.concatenate(probs, axis=0), vb)
        for g in range(GQA_GROUP):
            hq = h * GQA_GROUP + g
            og = oe[g * BLOCK:(g + 1) * BLOCK]
            denom = og[:, HEAD_DIM:HEAD_DIM + 1] + jnp.exp2(sinks[g] - maxes[g])
            o_ref[:, hq * HEAD_DIM:(hq + 1) * HEAD_DIM] = (og[:, :HEAD_DIM] / denom).astype(BF16)


def _attention(p, sink, n_rows):
    n_blocks = n_rows // BLOCK
    last = ROWS // BLOCK - 1
    kcol, vcol = K_OFF // 512, V_OFF // 512
    ctx_blk = SEQ // CTX_LEN
    kv = lambda col, off: pl.BlockSpec(
        (BLOCK, 512), lambda n, s: (jnp.clip(n + off, 0, last), col))
    return pl.pallas_call(
        _attn_kernel,
        grid_spec=pltpu.PrefetchScalarGridSpec(
            num_scalar_prefetch=1,
            grid=(n_blocks,),
            in_specs=[
                pl.BlockSpec((BLOCK, D_MODEL), lambda n, s: (n, 0)),
                kv(kcol, -1), kv(kcol, 0), kv(kcol, 1),
                kv(vcol, -1), kv(vcol, 0), kv(vcol, 1),
                pl.BlockSpec((CTX_LEN, 512), lambda n, s: (ctx_blk, kcol)),
                pl.BlockSpec((CTX_LEN, 512), lambda n, s: (ctx_blk, vcol)),
            ],
            out_specs=pl.BlockSpec((BLOCK, D_MODEL), lambda n, s: (n, 0)),
        ),
        out_shape=jax.ShapeDtypeStruct((n_rows, D_MODEL), BF16),
        compiler_params=_cparams(("arbitrary",)),
        name="attention",
    )(sink, p, p, p, p, p, p, p, p, p)


def _mix_kernel(x_ref, a_ref, u_ref, up_ref, un_ref, cb_ref, cc_ref, ccp_ref, ccn_ref,
                cx_ref, cxp_ref, cxn_ref, g0_ref, g1_ref, g2_ref, m_ref, pw_ref, ps_ref, cw_ref,
                wa_ref, wp_ref, wc_ref, wo_ref, o_ref, ub_ref):
    t = pl.program_id(0)
    tm = MIX_TM
    lat_tiles = SEQ // tm
    is_ctx = t >= lat_tiles
    seg_first = (t == 0) | is_ctx
    seg_last = (t == lat_tiles - 1) | is_ctx
    seg_len = jnp.where(is_ctx, CTX_LEN, SEQ)
    row = lax.broadcasted_iota(jnp.int32, (tm, 1), 0)
    pos = jnp.where(is_ctx, 0, t * tm) + row

    y_attn = _dot(a_ref[...], wa_ref[...])

    ub_ref[0:tm, :] = u_ref[...]
    ub_ref[tm:tm + HALO, :] = jnp.where(seg_last, jnp.zeros_like(un_ref[...]), un_ref[...])
    ub_ref[tm + HALO:POOL_K - HALO, :] = jnp.zeros((POOL_K - 2 * HALO - tm, POOL_WIDTH), BF16)
    ub_ref[POOL_K - HALO:POOL_K, :] = jnp.where(seg_first, jnp.zeros_like(up_ref[...]), up_ref[...])
    br = lax.broadcasted_iota(jnp.int32, (tm, POOL_K), 0)
    bc = lax.broadcasted_iota(jnp.int32, (tm, POOL_K), 1)
    d = bc - br
    d = jnp.where(d >= POOL_K // 2, d - POOL_K, d)
    pooled = []
    for gi, w in enumerate(POOL_WINDOWS):
        lo, hi = w // 2, w - w // 2 - 1
        gs = slice(gi * POOL_GROUP, (gi + 1) * POOL_GROUP)
        band = ((d >= -lo) & (d <= hi)).astype(BF16)
        wsum = _dot(band, ub_ref[:, gs])
        cnt = (jnp.minimum(pos + hi, seg_len - 1) - jnp.maximum(pos - lo, 0) + 1).astype(F32)
        y = wsum / cnt - u_ref[:, gs].astype(F32)
        pooled.append(_dot(y.astype(BF16), pw_ref[gi]))
    y_pool = jnp.concatenate(pooled, axis=1) * ps_ref[...]
    y_pool = _dot(y_pool.astype(BF16), wp_ref[...])

    z = cc_ref[...].astype(F32) * cx_ref[...].astype(F32)
    z_prev = ccp_ref[HALO - 1:HALO, :].astype(F32) * cxp_ref[HALO - 1:HALO, :].astype(F32)
    z_next = ccn_ref[0:1, :].astype(F32) * cxn_ref[0:1, :].astype(F32)
    z_prev = jnp.where(seg_first, 0.0, z_prev)
    z_next = jnp.where(seg_last, 0.0, z_next)
    zm1 = jnp.where(row == 0, z_prev, pltpu.roll(z, 1, 0))
    zp1 = jnp.where(row == tm - 1, z_next, pltpu.roll(z, tm - 1, 0))
    conv = zm1 * cw_ref[0:1, :] + z * cw_ref[1:2, :] + zp1 * cw_ref[2:3, :]
    y_conv = _dot((cb_ref[...].astype(F32) * conv).astype(BF16), wc_ref[...])

    merged = (g0_ref[...].astype(F32) * y_attn + g1_ref[...].astype(F32) * y_pool
              + g2_ref[...].astype(F32) * y_conv)
    out = _dot(merged.astype(BF16), wo_ref[...])
    o_ref[...] = x_ref[...] + _mod_rows(m_ref, 5, is_ctx) * out


def _mix(x, p, attn, mods, pool_w, pool_scale, conv_w, w_attn_out, w_pool_out, w_conv_out, w_o, n_rows):
    tm = MIX_TM
    n_tiles = n_rows // tm
    hb = tm // HALO
    last_halo = ROWS // HALO - 1
    tile = lambda width, col: pl.BlockSpec((tm, width), lambda t: (t, col))
    prev = lambda col: pl.BlockSpec((HALO, POOL_WIDTH), lambda t: (jnp.maximum(t * hb - 1, 0), col))
    nxt = lambda col: pl.BlockSpec((HALO, POOL_WIDTH), lambda t: (jnp.minimum((t + 1) * hb, last_halo), col))
    const = lambda shape: pl.BlockSpec(shape, lambda t: (0,) * len(shape), pipeline_mode=pl.Buffered(1))
    ucol, cbcol, cccol, cxcol = POOL_OFF // 1024, CB_OFF // 1024, CC_OFF // 1024, CX_OFF // 1024
    gcol = GATE_STORE_OFF // D_MODEL
    return pl.pallas_call(
        _mix_kernel,
        grid=(n_tiles,),
        in_specs=[
            tile(D_MODEL, 0),
            tile(D_MODEL, 0),
            tile(1024, ucol), prev(ucol), nxt(ucol),
            tile(1024, cbcol),
            tile(1024, cccol), prev(cccol), nxt(cccol),
            tile(1024, cxcol), prev(cxcol), nxt(cxcol),
            tile(D_MODEL, gcol), tile(D_MODEL, gcol + 1), tile(D_MODEL, gcol + 2),
            const((8, N_MODS * D_MODEL)),
            const((len(POOL_WINDOWS), POOL_GROUP, POOL_GROUP)),
            const((1, POOL_WIDTH)),
            const((3, CONV_WIDTH)),
            const((D_MODEL, D_MODEL)),
            const((POOL_WIDTH, D_MODEL)),
            const((CONV_WIDTH, D_MODEL)),
            const((D_MODEL, D_MODEL)),
        ],
        out_specs=pl.BlockSpec((tm, D_MODEL), lambda t: (t, 0)),
        out_shape=jax.ShapeDtypeStruct((n_rows, D_MODEL), F32),
        scratch_shapes=[pltpu.VMEM((POOL_K, POOL_WIDTH), BF16)],
        compiler_params=_cparams(("arbitrary",)),
        name="mix",
    )(x, attn, p, p, p, p, p, p, p, p, p, p, p, p, p, mods, pool_w, pool_scale.reshape(1, POOL_WIDTH),
      conv_w, w_attn_out, w_pool_out, w_conv_out, w_o)


def kernel(x, c, ctx, c_ctx, w_ada, b_ada, norm_g, ffn1_wi, ffn1_wo, w_in, attn_sink, pool_w, pool_scale,
           conv_w, w_attn_out, w_pool_out, w_conv_out, w_o, ffn2_wi, ffn2_wo, final_g):
    assert x.shape == (1, SEQ, D_MODEL) and ctx.shape == (1, CTX_LEN, D_MODEL)
    cond = jnp.concatenate([c.reshape(1, D_MODEL), c_ctx.reshape(1, D_MODEL),
                            jnp.zeros((6, D_MODEL), F32)], axis=0)
    mods = _ada(cond, w_ada, b_ada)
    cos, sin = _rope_tables()
    bf = lambda w: w.astype(BF16)
    ffn1_wi, ffn1_wo, ffn2_wi, ffn2_wo = bf(ffn1_wi), bf(ffn1_wo), bf(ffn2_wi), bf(ffn2_wo)
    w_in, pool_w = bf(w_in), bf(pool_w)
    w_attn_out, w_pool_out, w_conv_out, w_o = bf(w_attn_out), bf(w_pool_out), bf(w_conv_out), bf(w_o)

    h = jnp.concatenate([x[0], ctx[0]], axis=0)
    for l in range(DEPTH):
        last = l == DEPTH - 1
        m = mods[l]
        h = _ffn(h, ROWS, FFN_TM, FFN_TF, m, norm_g[l, 0], ffn1_wi, ffn1_wo, l, 0)
        p = _inproj(h, m, norm_g[l, 1], w_in, l, cos, sin)
        n_rows = SEQ if last else ROWS
        attn = _attention(p, attn_sink[l], n_rows)
        h = _mix(h, p, attn, m, pool_w[l], pool_scale[l], conv_w[l], w_attn_out[l], w_pool_out[l],
                 w_conv_out[l], w_o[l], n_rows)
        if last:
            h = _ffn(h, SEQ, FFN_TM_LATENT, FFN_TF_LATENT, m, norm_g[l, 2], ffn2_wi, ffn2_wo, l, 6,
                     final_g=final_g)
        else:
            h = _ffn(h, ROWS, FFN_TM, FFN_TF, m, norm_g[l, 2], ffn2_wi, ffn2_wo, l, 6)
    return h.reshape(1, SEQ, D_MODEL)
```

```python
import functools

import jax
import jax.numpy as jnp
from jax import lax
from jax.experimental import pallas as pl
from jax.experimental.pallas import tpu as pltpu

D_MODEL = 2048
SEQ = 8192
CTX_LEN = 256
ROWS = SEQ + CTX_LEN
DEPTH = 2
GRID_W = 64
HEAD_DIM = 128
N_Q_HEADS = 16
N_KV_HEADS = 4
GQA_GROUP = 4
WINDOW = 128
BLOCK = 128
ROPE_PAIRS = 32
ROPE_BASE = 10000.0
POOL_WINDOWS = (2, 4, 8, 16)
POOL_WIDTH = 1024
POOL_GROUP = 256
CONV_WIDTH = 1024
D_FF = 5632
N_MODS = 9
EPS = 1e-6
NEG_INF = -1e30
LOG2E = 1.4426950408889634

K_OFF = 2048
V_OFF = 2560
POOL_OFF = 3072
CB_OFF = 4096
CC_OFF = 5120
CX_OFF = 6144
GATE_OFF = 7168
IN_COLS = GATE_OFF + 3 * D_MODEL

BF16 = jnp.bfloat16
F32 = jnp.float32

VMEM_LIMIT_BYTES = 58 * 1024 * 1024

FFN_TM = 768
FFN_TM_LATENT = 512
FFN_TF = 512
FFN_TF_CAST = 256
PROJ_TN = 1024
PROJ_CHUNK = 512
ROW_CHUNK = 32
ROWS_UNROLL = 4
COL_CHUNK = 512
MIX_TM = 256
GATE_TN = 1024
HALO = 16
POOL_K = 384


def _cparams(sem):
    return pltpu.CompilerParams(dimension_semantics=sem, vmem_limit_bytes=VMEM_LIMIT_BYTES)


def _dot(a, b):
    return jnp.dot(a, b, preferred_element_type=F32)


def _norm_mod(x, g, shift, scale):
    y = x * lax.rsqrt(jnp.mean(x * x, axis=-1, keepdims=True) + EPS)
    return y * (g * (1.0 + scale)) + shift


def _mod_rows(m_ref, k, is_ctx):
    lo, hi = k * D_MODEL, (k + 1) * D_MODEL
    return jnp.where(is_ctx, m_ref[1:2, lo:hi], m_ref[0:1, lo:hi])


assert SEQ % ROW_CHUNK == 0


def _is_ctx_rows(first_row):
    return first_row >= SEQ


def _rows_loop(tm, body, unroll):
    def step(r, carry):
        body(pl.multiple_of(r * ROW_CHUNK, ROW_CHUNK))
        return carry
    lax.fori_loop(0, tm // ROW_CHUNK, step, 0, unroll=unroll)


def _norm_mod_rows(x_ref, xm_ref, g_ref, m_ref, k_shift, row0, tm):
    def rows(r0):
        rs = pl.ds(r0, ROW_CHUNK)
        is_ctx = _is_ctx_rows(row0 + r0)
        xm_ref[rs, :] = _norm_mod(x_ref[rs, :], g_ref[...], _mod_rows(m_ref, k_shift, is_ctx),
                                  _mod_rows(m_ref, k_shift + 1, is_ctx)).astype(BF16)
    _rows_loop(tm, rows, ROWS_UNROLL)


def _ada_kernel(c_ref, w_ref, b_ref, o_ref):
    c = c_ref[...]
    s = c * jax.nn.sigmoid(c)
    o_ref[...] = _dot(s.astype(BF16), w_ref[...].astype(BF16)) + b_ref[...]


def _ada(cond, w_ada, b_ada):
    tn = 1024
    n_cols = N_MODS * D_MODEL
    return pl.pallas_call(
        _ada_kernel,
        grid=(DEPTH, n_cols // tn),
        in_specs=[
            pl.BlockSpec((8, D_MODEL), lambda l, j: (0, 0)),
            pl.BlockSpec((None, D_MODEL, tn), lambda l, j: (l, 0, j)),
            pl.BlockSpec((None, 1, tn), lambda l, j: (l, 0, j)),
        ],
        out_specs=pl.BlockSpec((None, 8, tn), lambda l, j: (l, 0, j)),
        out_shape=jax.ShapeDtypeStruct((DEPTH, 8, n_cols), F32),
        compiler_params=_cparams(("arbitrary", "arbitrary")),
        name="ada",
    )(cond, w_ada, b_ada.reshape(DEPTH, 1, n_cols))


def _ffn_kernel(*refs, tm, tile0, k0, final, cast):
    if cast:
        (x_ref, m_ref, g_ref, wig_in, wiu_in, wo_in, fg_ref,
         o_ref, wig_ref, wiu_ref, wo_ref, xm_ref, acc_ref) = refs
    else:
        x_ref, m_ref, g_ref, wig_ref, wiu_ref, wo_ref, fg_ref, o_ref, xm_ref, acc_ref = refs
    row0 = (pl.program_id(0) + tile0) * tm
    f = pl.program_id(1)

    @pl.when(f == 0)
    def _():
        _norm_mod_rows(x_ref, xm_ref, g_ref, m_ref, k0, row0, tm)
        acc_ref[...] = jnp.zeros_like(acc_ref)

    if cast:
        wig_ref[...] = wig_in[...].astype(BF16)
        wiu_ref[...] = wiu_in[...].astype(BF16)
        wo_ref[...] = wo_in[...].astype(BF16)
    xm = xm_ref[...]
    g = _dot(xm, wig_ref[...])
    u = _dot(xm, wiu_ref[...])
    a = ((g * jax.nn.sigmoid(g)) * u).astype(BF16)
    for c0 in range(0, D_MODEL, COL_CHUNK):
        acc_ref[:, c0:c0 + COL_CHUNK] += _dot(a, wo_ref[:, c0:c0 + COL_CHUNK])

    @pl.when(f == pl.num_programs(1) - 1)
    def _():
        def rows(r0):
            rs = pl.ds(r0, ROW_CHUNK)
            gate = _mod_rows(m_ref, k0 + 2, _is_ctx_rows(row0 + r0))
            y = x_ref[rs, :] + (0.5 * gate) * acc_ref[rs, :]
            if final:
                y = (y * lax.rsqrt(jnp.mean(y * y, axis=-1, keepdims=True) + EPS)) * fg_ref[...]
            o_ref[rs, :] = y
        _rows_loop(tm, rows, ROWS_UNROLL if final else 1)


def _ffn(x, tm, tf, mods, norm_g, wi, wo, layer, k0, final_g, final=False):
    n_rows = x.shape[0]
    kern = functools.partial(_ffn_kernel, tm=tm, k0=k0, final=final)
    vec = lambda i, f: (0, 0)
    small = [pl.BlockSpec((8, N_MODS * D_MODEL), vec), pl.BlockSpec((1, D_MODEL), vec)]
    scratch = [pltpu.VMEM((tm, D_MODEL), BF16), pltpu.VMEM((tm, D_MODEL), F32)]
    g2, fg2 = norm_g.reshape(1, D_MODEL), final_g.reshape(1, D_MODEL)

    tc = FFN_TF_CAST
    n_fc = D_FF // tc
    x, wig, wiu, wo_bf = pl.pallas_call(
        functools.partial(kern, tile0=0, cast=True),
        grid=(1, n_fc),
        in_specs=[pl.BlockSpec((tm, D_MODEL), lambda i, f: (0, 0))] + small + [
            pl.BlockSpec((None, D_MODEL, tc), lambda i, f: (layer, 0, f)),
            pl.BlockSpec((None, D_MODEL, tc), lambda i, f: (layer, 0, n_fc + f)),
            pl.BlockSpec((None, tc, D_MODEL), lambda i, f: (layer, f, 0)),
            pl.BlockSpec((1, D_MODEL), vec)],
        out_specs=[pl.BlockSpec((tm, D_MODEL), lambda i, f: (0, 0)),
                   pl.BlockSpec((D_MODEL, tc), lambda i, f: (0, f)),
                   pl.BlockSpec((D_MODEL, tc), lambda i, f: (0, f)),
                   pl.BlockSpec((tc, D_MODEL), lambda i, f: (f, 0))],
        out_shape=[jax.ShapeDtypeStruct((n_rows, D_MODEL), F32),
                   jax.ShapeDtypeStruct((D_MODEL, D_FF), BF16),
                   jax.ShapeDtypeStruct((D_MODEL, D_FF), BF16),
                   jax.ShapeDtypeStruct((D_FF, D_MODEL), BF16)],
        input_output_aliases={0: 0},
        scratch_shapes=scratch,
        compiler_params=_cparams(("arbitrary", "arbitrary")),
        name="ffn_cast_final" if final else "ffn_cast",
    )(x, mods, g2, wi, wi, wo, fg2)

    return pl.pallas_call(
        functools.partial(kern, tile0=1, cast=False),
        grid=(n_rows // tm - 1, D_FF // tf),
        in_specs=[pl.BlockSpec((tm, D_MODEL), lambda i, f: (i + 1, 0))] + small + [
            pl.BlockSpec((D_MODEL, tf), lambda i, f: (0, f)),
            pl.BlockSpec((D_MODEL, tf), lambda i, f: (0, f)),
            pl.BlockSpec((tf, D_MODEL), lambda i, f: (f, 0)),
            pl.BlockSpec((1, D_MODEL), vec)],
        out_specs=pl.BlockSpec((tm, D_MODEL), lambda i, f: (i + 1, 0)),
        out_shape=jax.ShapeDtypeStruct((n_rows, D_MODEL), F32),
        input_output_aliases={0: 0},
        scratch_shapes=scratch,
        compiler_params=_cparams(("arbitrary", "arbitrary")),
        name="ffn_final" if final else "ffn",
    )(x, mods, g2, wig, wiu, wo_bf, fg2)


def _rope(t, cos, sin):
    lane = lax.broadcasted_iota(jnp.int32, (1, HEAD_DIM), 1)
    first_half = (lane % (2 * ROPE_PAIRS)) < ROPE_PAIRS
    outs = []
    for hd in range(t.shape[1] // HEAD_DIM):
        th = t[:, hd * HEAD_DIM:(hd + 1) * HEAD_DIM]
        partner = jnp.where(first_half, pltpu.roll(th, HEAD_DIM - ROPE_PAIRS, 1),
                            pltpu.roll(th, ROPE_PAIRS, 1))
        outs.append(th * cos + partner * sin)
    return jnp.concatenate(outs, axis=1)


def _inproj_kernel(*refs, tm, cast):
    if cast:
        x_ref, m_ref, g_ref, w_in, cos_ref, sin_ref, o_ref, w_ref, xn_ref = refs
    else:
        x_ref, m_ref, g_ref, w_ref, cos_ref, sin_ref, p0_ref, o_ref, xn_ref = refs
    i = pl.program_id(0)
    j = pl.program_id(1)
    q_blocks = K_OFF // PROJ_TN
    plain_end = GATE_OFF // PROJ_TN

    def project(epilogue):
        for c0 in range(0, PROJ_TN, PROJ_CHUNK):
            cs = slice(c0, c0 + PROJ_CHUNK)
            o_ref[:, cs] = epilogue(_dot(xn_ref[...], w_ref[:, cs]), c0).astype(BF16)

    def rope(acc):
        return _rope(acc, cos_ref[...], sin_ref[...])

    def compute():
        @pl.when(j == 0)
        def _():
            _norm_mod_rows(x_ref, xn_ref, g_ref, m_ref, 3, i * tm, tm)

        if cast:
            w_ref[...] = w_in[...].astype(BF16)

        @pl.when(j < q_blocks)
        def _():
            project(lambda acc, c0: rope(acc) * (HEAD_DIM ** -0.5 * LOG2E))

        @pl.when(j == q_blocks)
        def _():
            project(lambda acc, c0: rope(acc) if c0 < V_OFF - K_OFF else acc)

        @pl.when((j > q_blocks) & (j < plain_end))
        def _():
            project(lambda acc, c0: acc)

        @pl.when(j >= plain_end)
        def _():
            project(lambda acc, c0: jax.nn.sigmoid(acc))

    if cast:
        compute()
    else:
        @pl.when(i == 0)
        def _():
            o_ref[...] = p0_ref[...]

        pl.when(i > 0)(compute)


def _inproj(x, mods, norm_g, w_in, layer, cos, sin):
    tm = FFN_TM
    n_j = IN_COLS // PROJ_TN
    vec = lambda i, j: (0, 0)
    small = [pl.BlockSpec((8, N_MODS * D_MODEL), vec), pl.BlockSpec((1, D_MODEL), vec)]
    row = lambda width: pl.BlockSpec((tm, width), lambda i, j: (i, 0))
    scratch = [pltpu.VMEM((tm, D_MODEL), BF16)]
    g2 = norm_g.reshape(1, D_MODEL)

    p0, w_bf = pl.pallas_call(
        functools.partial(_inproj_kernel, tm=tm, cast=True),
        grid=(1, n_j),
        in_specs=[row(D_MODEL)] + small + [
            pl.BlockSpec((None, D_MODEL, PROJ_TN), lambda i, j: (layer, 0, j)),
            row(HEAD_DIM), row(HEAD_DIM)],
        out_specs=[pl.BlockSpec((tm, PROJ_TN), lambda i, j: (0, j)),
                   pl.BlockSpec((D_MODEL, PROJ_TN), lambda i, j: (0, j))],
        out_shape=[jax.ShapeDtypeStruct((tm, IN_COLS), BF16),
                   jax.ShapeDtypeStruct((D_MODEL, IN_COLS), BF16)],
        scratch_shapes=scratch,
        compiler_params=_cparams(("arbitrary", "arbitrary")),
        name="inproj_cast",
    )(x, mods, g2, w_in, cos, sin)

    return pl.pallas_call(
        functools.partial(_inproj_kernel, tm=tm, cast=False),
        grid=(ROWS // tm, n_j),
        in_specs=[row(D_MODEL)] + small + [
            pl.BlockSpec((D_MODEL, PROJ_TN), lambda i, j: (0, jnp.where(i == 0, 0, j))),
            row(HEAD_DIM), row(HEAD_DIM),
            pl.BlockSpec((tm, PROJ_TN), lambda i, j: (0, jnp.where(i == 0, j, n_j - 1)))],
        out_specs=pl.BlockSpec((tm, PROJ_TN), lambda i, j: (i, j)),
        out_shape=jax.ShapeDtypeStruct((ROWS, IN_COLS), BF16),
        scratch_shapes=scratch,
        compiler_params=_cparams(("arbitrary", "arbitrary")),
        name="inproj",
    )(x, mods, g2, w_bf, cos, sin, p0)


def _rope_tables():
    pos = jnp.arange(SEQ, dtype=jnp.int32)
    r = (pos // GRID_W).astype(F32)
    col = (pos % GRID_W).astype(F32)
    inv = ROPE_BASE ** (-jnp.arange(ROPE_PAIRS, dtype=F32) / ROPE_PAIRS)
    ang_r, ang_c = r[:, None] * inv, col[:, None] * inv
    cos = jnp.concatenate([jnp.cos(ang_r), jnp.cos(ang_r), jnp.cos(ang_c), jnp.cos(ang_c)], axis=1)
    sin = jnp.concatenate([-jnp.sin(ang_r), jnp.sin(ang_r), -jnp.sin(ang_c), jnp.sin(ang_c)], axis=1)
    cos = jnp.concatenate([cos, jnp.ones((CTX_LEN, HEAD_DIM), F32)], axis=0)
    sin = jnp.concatenate([sin, jnp.zeros((CTX_LEN, HEAD_DIM), F32)], axis=0)
    return cos, sin


def _attn_kernel(sink_ref, q_ref, kp_ref, kc_ref, kn_ref, vp_ref, vc_ref, vn_ref, kx_ref, vx_ref, o_ref):
    n = pl.program_id(0)
    n_lat = SEQ // BLOCK
    r = lax.broadcasted_iota(jnp.int32, (BLOCK, BLOCK), 0)
    c = lax.broadcasted_iota(jnp.int32, (BLOCK, BLOCK), 1)
    is_lat = n < n_lat
    prev_ok = (c >= r) & (n >= 1) & is_lat
    next_ok = (c <= r) & (n < n_lat - 1)
    n_keys = 3 * BLOCK + CTX_LEN
    ones = jnp.ones((n_keys, HEAD_DIM), BF16)
    contract_last = (((1,), (1,)), ((), ()))
    for h in range(N_KV_HEADS):
        hs = slice(h * HEAD_DIM, (h + 1) * HEAD_DIM)
        qg = jnp.concatenate(
            [q_ref[:, (h * GQA_GROUP + g) * HEAD_DIM:(h * GQA_GROUP + g + 1) * HEAD_DIM]
             for g in range(GQA_GROUP)], axis=0)
        kb = jnp.concatenate([kp_ref[:, hs], kc_ref[:, hs], kn_ref[:, hs], kx_ref[:, hs]], axis=0)
        vb = jnp.concatenate([vp_ref[:, hs], vc_ref[:, hs], vn_ref[:, hs], vx_ref[:, hs]], axis=0)
        vb = jnp.concatenate([vb, ones], axis=1)
        s = lax.dot_general(qg, kb, contract_last, preferred_element_type=F32)
        probs, maxes, sinks = [], [], []
        for g in range(GQA_GROUP):
            sg = s[g * BLOCK:(g + 1) * BLOCK]
            parts = [jnp.where(prev_ok, sg[:, 0:BLOCK], NEG_INF),
                     jnp.where(is_lat, sg[:, BLOCK:2 * BLOCK], NEG_INF),
                     jnp.where(next_ok, sg[:, 2 * BLOCK:3 * BLOCK], NEG_INF),
                     sg[:, 3 * BLOCK:4 * BLOCK], sg[:, 4 * BLOCK:5 * BLOCK]]
            sink = sink_ref[h * GQA_GROUP + g] * LOG2E
            lane_max = jnp.maximum(jnp.maximum(jnp.maximum(parts[0], parts[1]),
                                               jnp.maximum(parts[2], parts[3])), parts[4])
            m = jnp.maximum(jnp.max(lane_max, axis=-1, keepdims=True), sink)
            probs.append(jnp.concatenate([jnp.exp2(pt - m).astype(BF16) for pt in parts], axis=1))
            maxes.append(m)
            sinks.append(sink)
        oe = _dot(jnp.concatenate(probs, axis=0), vb)
        for g in range(GQA_GROUP):
            hq = h * GQA_GROUP + g
            og = oe[g * BLOCK:(g + 1) * BLOCK]
            denom = og[:, HEAD_DIM:HEAD_DIM + 1] + jnp.exp2(sinks[g] - maxes[g])
            o_ref[:, hq * HEAD_DIM:(hq + 1) * HEAD_DIM] = (og[:, :HEAD_DIM] / denom).astype(BF16)


def _attention(p, sink, n_rows):
    n_blocks = n_rows // BLOCK
    last = ROWS // BLOCK - 1
    kcol, vcol = K_OFF // 512, V_OFF // 512
    ctx_blk = SEQ // CTX_LEN
    kv = lambda col, off: pl.BlockSpec(
        (BLOCK, 512), lambda n, s: (jnp.clip(n + off, 0, last), col))
    return pl.pallas_call(
        _attn_kernel,
        grid_spec=pltpu.PrefetchScalarGridSpec(
            num_scalar_prefetch=1,
            grid=(n_blocks,),
            in_specs=[
                pl.BlockSpec((BLOCK, D_MODEL), lambda n, s: (n, 0)),
                kv(kcol, -1), kv(kcol, 0), kv(kcol, 1),
                kv(vcol, -1), kv(vcol, 0), kv(vcol, 1),
                pl.BlockSpec((CTX_LEN, 512), lambda n, s: (ctx_blk, kcol)),
                pl.BlockSpec((CTX_LEN, 512), lambda n, s: (ctx_blk, vcol)),
            ],
            out_specs=pl.BlockSpec((BLOCK, D_MODEL), lambda n, s: (n, 0)),
        ),
        out_shape=jax.ShapeDtypeStruct((n_rows, D_MODEL), BF16),
        compiler_params=_cparams(("arbitrary",)),
        name="attention",
    )(sink, p, p, p, p, p, p, p, p, p)


def _mix_kernel(x_ref, a_ref, u_ref, up_ref, un_ref, cb_ref, cc_ref, ccp_ref, ccn_ref,
                cx_ref, cxp_ref, cxn_ref, g0a_ref, g0b_ref, g1a_ref, g1b_ref, g2a_ref, g2b_ref,
                m_ref, pw_ref, ps_ref, cw_ref, wa_ref, wp_ref, wc_ref, wo_ref, o_ref, ub_ref):
    t = pl.program_id(0)
    tm = MIX_TM
    lat_tiles = SEQ // tm
    is_ctx = t >= lat_tiles
    seg_first = (t == 0) | is_ctx
    seg_last = (t == lat_tiles - 1) | is_ctx
    seg_len = jnp.where(is_ctx, CTX_LEN, SEQ)
    row = lax.broadcasted_iota(jnp.int32, (tm, 1), 0)
    pos = jnp.where(is_ctx, 0, t * tm) + row

    y_attn = _dot(a_ref[...], wa_ref[...])

    ub_ref[0:tm, :] = u_ref[...]
    ub_ref[tm:tm + HALO, :] = jnp.where(seg_last, jnp.zeros_like(un_ref[...]), un_ref[...])
    ub_ref[tm + HALO:POOL_K - HALO, :] = jnp.zeros((POOL_K - 2 * HALO - tm, POOL_WIDTH), BF16)
    ub_ref[POOL_K - HALO:POOL_K, :] = jnp.where(seg_first, jnp.zeros_like(up_ref[...]), up_ref[...])
    br = lax.broadcasted_iota(jnp.int32, (tm, POOL_K), 0)
    bc = lax.broadcasted_iota(jnp.int32, (tm, POOL_K), 1)
    d = bc - br
    d = jnp.where(d >= POOL_K // 2, d - POOL_K, d)
    pooled = []
    for gi, w in enumerate(POOL_WINDOWS):
        lo, hi = w // 2, w - w // 2 - 1
        gs = slice(gi * POOL_GROUP, (gi + 1) * POOL_GROUP)
        band = ((d >= -lo) & (d <= hi)).astype(BF16)
        wsum = _dot(band, ub_ref[:, gs])
        cnt = (jnp.minimum(pos + hi, seg_len - 1) - jnp.maximum(pos - lo, 0) + 1).astype(F32)
        y = wsum / cnt - u_ref[:, gs].astype(F32)
        pooled.append(_dot(y.astype(BF16), pw_ref[gi]))
    y_pool = jnp.concatenate(pooled, axis=1) * ps_ref[...]
    y_pool = _dot(y_pool.astype(BF16), wp_ref[...])

    z = cc_ref[...].astype(F32) * cx_ref[...].astype(F32)
    z_prev = ccp_ref[HALO - 1:HALO, :].astype(F32) * cxp_ref[HALO - 1:HALO, :].astype(F32)
    z_next = ccn_ref[0:1, :].astype(F32) * cxn_ref[0:1, :].astype(F32)
    z_prev = jnp.where(seg_first, 0.0, z_prev)
    z_next = jnp.where(seg_last, 0.0, z_next)
    zm1 = jnp.where(row == 0, z_prev, pltpu.roll(z, 1, 0))
    zp1 = jnp.where(row == tm - 1, z_next, pltpu.roll(z, tm - 1, 0))
    conv = zm1 * cw_ref[0:1, :] + z * cw_ref[1:2, :] + zp1 * cw_ref[2:3, :]
    y_conv = _dot((cb_ref[...].astype(F32) * conv).astype(BF16), wc_ref[...])

    halves = []
    for k, (g0, g1, g2) in enumerate(((g0a_ref, g1a_ref, g2a_ref), (g0b_ref, g1b_ref, g2b_ref))):
        cs = slice(k * GATE_TN, (k + 1) * GATE_TN)
        halves.append((g0[...].astype(F32) * y_attn[:, cs] + g1[...].astype(F32) * y_pool[:, cs]
                       + g2[...].astype(F32) * y_conv[:, cs]).astype(BF16))
    out = _dot(jnp.concatenate(halves, axis=1), wo_ref[...])
    o_ref[...] = x_ref[...] + _mod_rows(m_ref, 5, is_ctx) * out


def _mix(x, p, attn, mods, pool_w, pool_scale, conv_w, w_attn_out, w_pool_out, w_conv_out, w_o, n_rows):
    tm = MIX_TM
    n_tiles = n_rows // tm
    hb = tm // HALO
    last_halo = ROWS // HALO - 1
    tile = lambda width, col: pl.BlockSpec((tm, width), lambda t: (t, col))
    prev = lambda col: pl.BlockSpec((HALO, POOL_WIDTH), lambda t: (jnp.maximum(t * hb - 1, 0), col))
    nxt = lambda col: pl.BlockSpec((HALO, POOL_WIDTH), lambda t: (jnp.minimum((t + 1) * hb, last_halo), col))
    const = lambda shape: pl.BlockSpec(shape, lambda t: (0,) * len(shape), pipeline_mode=pl.Buffered(1))
    ucol, cbcol, cccol, cxcol = POOL_OFF // 1024, CB_OFF // 1024, CC_OFF // 1024, CX_OFF // 1024
    gcol = GATE_OFF // GATE_TN
    return pl.pallas_call(
        _mix_kernel,
        grid=(n_tiles,),
        in_specs=[
            tile(D_MODEL, 0),
            tile(D_MODEL, 0),
            tile(1024, ucol), prev(ucol), nxt(ucol),
            tile(1024, cbcol),
            tile(1024, cccol), prev(cccol), nxt(cccol),
            tile(1024, cxcol), prev(cxcol), nxt(cxcol),
            *[tile(GATE_TN, gcol + k) for k in range(3 * D_MODEL // GATE_TN)],
            const((8, N_MODS * D_MODEL)),
            const((len(POOL_WINDOWS), POOL_GROUP, POOL_GROUP)),
            const((1, POOL_WIDTH)),
            const((3, CONV_WIDTH)),
            const((D_MODEL, D_MODEL)),
            const((POOL_WIDTH, D_MODEL)),
            const((CONV_WIDTH, D_MODEL)),
            const((D_MODEL, D_MODEL)),
        ],
        out_specs=pl.BlockSpec((tm, D_MODEL), lambda t: (t, 0)),
        out_shape=jax.ShapeDtypeStruct((n_rows, D_MODEL), F32),
        scratch_shapes=[pltpu.VMEM((POOL_K, POOL_WIDTH), BF16)],
        compiler_params=_cparams(("arbitrary",)),
        name="mix",
    )(x, attn, *([p] * 16), mods, pool_w, pool_scale.reshape(1, POOL_WIDTH),
      conv_w, w_attn_out, w_pool_out, w_conv_out, w_o)


def kernel(x, c, ctx, c_ctx, w_ada, b_ada, norm_g, ffn1_wi, ffn1_wo, w_in, attn_sink, pool_w, pool_scale,
           conv_w, w_attn_out, w_pool_out, w_conv_out, w_o, ffn2_wi, ffn2_wo, final_g):
    assert x.shape == (1, SEQ, D_MODEL) and ctx.shape == (1, CTX_LEN, D_MODEL)
    cond = jnp.concatenate([c.reshape(1, D_MODEL), c_ctx.reshape(1, D_MODEL),
                            jnp.zeros((6, D_MODEL), F32)], axis=0)
    mods = _ada(cond, w_ada, b_ada)
    cos, sin = _rope_tables()
    bf = lambda w: w.astype(BF16)
    pool_w, w_attn_out, w_pool_out, w_conv_out, w_o = (
        bf(pool_w), bf(w_attn_out), bf(w_pool_out), bf(w_conv_out), bf(w_o))

    h = jnp.concatenate([x[0], ctx[0]], axis=0)
    for l in range(DEPTH):
        last = l == DEPTH - 1
        m = mods[l]
        h = _ffn(h, FFN_TM, FFN_TF, m, norm_g[l, 0], ffn1_wi, ffn1_wo, l, 0, final_g)
        p = _inproj(h, m, norm_g[l, 1], w_in, l, cos, sin)
        n_rows = SEQ if last else ROWS
        attn = _attention(p, attn_sink[l], n_rows)
        h = _mix(h, p, attn, m, pool_w[l], pool_scale[l], conv_w[l], w_attn_out[l], w_pool_out[l],
                 w_conv_out[l], w_o[l], n_rows)
        if last:
            h = _ffn(h, FFN_TM_LATENT, FFN_TF, m, norm_g[l, 2], ffn2_wi, ffn2_wo, l, 6, final_g, final=True)
        else:
            h = _ffn(h, FFN_TM, FFN_TF, m, norm_g[l, 2], ffn2_wi, ffn2_wo, l, 6, final_g)
    return h.reshape(1, SEQ, D_MODEL)
```

```python
import functools

import jax
import jax.numpy as jnp
import numpy as np
from jax import lax
from jax.experimental import pallas as pl
from jax.experimental.pallas import tpu as pltpu

D_MODEL = 2048
SEQ = 8192
CTX_LEN = 256
ROWS = SEQ + CTX_LEN
DEPTH = 2
GRID_W = 64
HEAD_DIM = 128
N_Q_HEADS = 16
N_KV_HEADS = 4
GQA_GROUP = 4
WINDOW = 128
BLOCK = 128
ROPE_PAIRS = 32
ROPE_BASE = 10000.0
POOL_WINDOWS = (2, 4, 8, 16)
POOL_WIDTH = 1024
POOL_GROUP = 256
CONV_WIDTH = 1024
D_FF = 5632
N_MODS = 9
EPS = 1e-6
NEG_INF = -1e30
LOG2E = 1.4426950408889634

K_OFF = 2048
V_OFF = 2560
POOL_OFF = 3072
CB_OFF = 4096
CC_OFF = 5120
CX_OFF = 6144
GATE_OFF = 7168
IN_COLS = GATE_OFF + 3 * D_MODEL

BF16 = jnp.bfloat16
F32 = jnp.float32

VMEM_LIMIT_BYTES = 60 * 1024 * 1024

FFN_TM = 1056
FFN_TM_LATENT = 1024
PROJ_TM = 1408
FFN_TF = 512
FFN_TF_CAST = 256
PROJ_TN = 1024
PROJ_CHUNK = 512
ROW_CHUNK = 32
ROWS_UNROLL = 4
COL_CHUNK = 512
MIX_TM = 256
MIX_WEIGHT_ROWS = (2048, 1024, 1024, 2048, 1024)
GATE_TN = 1024
HALO = 16
POOL_K = 384


def _cparams(sem):
    return pltpu.CompilerParams(dimension_semantics=sem, vmem_limit_bytes=VMEM_LIMIT_BYTES)


def _layer_vector_specs(layer, norm_idx):
    return [pl.BlockSpec((None, 8, N_MODS * D_MODEL), lambda *_: (layer, 0, 0)),
            pl.BlockSpec((None, 1, D_MODEL), lambda *_: (layer * 3 + norm_idx, 0, 0))]


def _dot(a, b):
    return jnp.dot(a, b, preferred_element_type=F32)


def _norm_mod(x, g, shift, scale):
    y = x * lax.rsqrt(jnp.mean(x * x, axis=-1, keepdims=True) + EPS)
    return y * (g * (1.0 + scale)) + shift


def _mod_rows(m_ref, k, is_ctx):
    lo, hi = k * D_MODEL, (k + 1) * D_MODEL
    return jnp.where(is_ctx, m_ref[1:2, lo:hi], m_ref[0:1, lo:hi])


assert SEQ % ROW_CHUNK == 0


def _is_ctx_rows(first_row):
    return first_row >= SEQ


def _rows_loop(tm, body, unroll):
    def step(r, carry):
        body(pl.multiple_of(r * ROW_CHUNK, ROW_CHUNK))
        return carry
    lax.fori_loop(0, tm // ROW_CHUNK, step, 0, unroll=unroll)


def _norm_mod_rows(x_ref, xm_ref, g_ref, m_ref, k_shift, row0, tm):
    def rows(r0):
        rs = pl.ds(r0, ROW_CHUNK)
        is_ctx = _is_ctx_rows(row0 + r0)
        xm_ref[rs, :] = _norm_mod(x_ref[rs, :], g_ref[...], _mod_rows(m_ref, k_shift, is_ctx),
                                  _mod_rows(m_ref, k_shift + 1, is_ctx)).astype(BF16)
    _rows_loop(tm, rows, ROWS_UNROLL)


def _ada_kernel(c_ref, w_ref, b_ref, o_ref):
    c = c_ref[...]
    s = c * jax.nn.sigmoid(c)
    o_ref[...] = _dot(s.astype(BF16), w_ref[...].astype(BF16)) + b_ref[...]


def _ada(cond, w_ada, b_ada):
    tn = 1024
    n_cols = N_MODS * D_MODEL
    return pl.pallas_call(
        _ada_kernel,
        grid=(DEPTH, n_cols // tn),
        in_specs=[
            pl.BlockSpec((8, D_MODEL), lambda l, j: (0, 0)),
            pl.BlockSpec((None, D_MODEL, tn), lambda l, j: (l, 0, j)),
            pl.BlockSpec((None, 1, tn), lambda l, j: (l, 0, j)),
        ],
        out_specs=pl.BlockSpec((None, 8, tn), lambda l, j: (l, 0, j)),
        out_shape=jax.ShapeDtypeStruct((DEPTH, 8, n_cols), F32),
        compiler_params=_cparams(("arbitrary", "arbitrary")),
        name="ada",
    )(cond, w_ada, b_ada.reshape(DEPTH, 1, n_cols))


def _ffn_kernel(*refs, tm, tile0, k0, final, cast):
    if cast:
        (x_ref, m_ref, g_ref, wig_in, wiu_in, wo_in, fg_ref,
         o_ref, wig_ref, wiu_ref, wo_ref, xm_ref) = refs
    else:
        x_ref, m_ref, g_ref, wig_ref, wiu_ref, wo_ref, fg_ref, o_ref, xm_ref = refs
    row0 = (pl.program_id(0) + tile0) * tm
    f = pl.program_id(1)

    @pl.when(f == 0)
    def _():
        _norm_mod_rows(x_ref, xm_ref, g_ref, m_ref, k0, row0, tm)
        o_ref[...] = jnp.zeros_like(o_ref)

    if cast:
        wig_ref[...] = wig_in[...].astype(BF16)
        wiu_ref[...] = wiu_in[...].astype(BF16)
        wo_ref[...] = wo_in[...].astype(BF16)
    xm = xm_ref[...]
    g = _dot(xm, wig_ref[...])
    u = _dot(xm, wiu_ref[...])
    a = ((g * jax.nn.sigmoid(g)) * u).astype(BF16)
    for c0 in range(0, D_MODEL, COL_CHUNK):
        o_ref[:, c0:c0 + COL_CHUNK] += _dot(a, wo_ref[:, c0:c0 + COL_CHUNK])

    @pl.when(f == pl.num_programs(1) - 1)
    def _():
        def rows(r0):
            rs = pl.ds(r0, ROW_CHUNK)
            gate = _mod_rows(m_ref, k0 + 2, _is_ctx_rows(row0 + r0))
            y = x_ref[rs, :] + (0.5 * gate) * o_ref[rs, :]
            if final:
                y = (y * lax.rsqrt(jnp.mean(y * y, axis=-1, keepdims=True) + EPS)) * fg_ref[...]
            o_ref[rs, :] = y
        _rows_loop(tm, rows, ROWS_UNROLL if final else 1)


def _ffn(x, tm, tf, mods, norm_g, wi, wo, layer, k0, final_g, final=False):
    n_rows = x.shape[0]
    kern = functools.partial(_ffn_kernel, tm=tm, k0=k0, final=final)
    vec = lambda i, f: (0, 0)
    small = _layer_vector_specs(layer, k0 // 3)
    scratch = [pltpu.VMEM((tm, D_MODEL), BF16)]
    g2, fg2 = norm_g.reshape(DEPTH * 3, 1, D_MODEL), final_g.reshape(1, D_MODEL)

    tc = FFN_TF_CAST
    n_fc = D_FF // tc
    x, wig, wiu, wo_bf = pl.pallas_call(
        functools.partial(kern, tile0=0, cast=True),
        grid=(1, n_fc),
        in_specs=[pl.BlockSpec((tm, D_MODEL), lambda i, f: (0, 0), pipeline_mode=pl.Buffered(1))] + small + [
            pl.BlockSpec((None, D_MODEL, tc), lambda i, f: (layer, 0, f)),
            pl.BlockSpec((None, D_MODEL, tc), lambda i, f: (layer, 0, n_fc + f)),
            pl.BlockSpec((None, tc, D_MODEL), lambda i, f: (layer, f, 0)),
            pl.BlockSpec((1, D_MODEL), vec)],
        out_specs=[pl.BlockSpec((tm, D_MODEL), lambda i, f: (0, 0)),
                   pl.BlockSpec((D_MODEL, tc), lambda i, f: (0, f)),
                   pl.BlockSpec((D_MODEL, tc), lambda i, f: (0, f)),
                   pl.BlockSpec((tc, D_MODEL), lambda i, f: (f, 0))],
        out_shape=[jax.ShapeDtypeStruct((n_rows, D_MODEL), F32),
                   jax.ShapeDtypeStruct((D_MODEL, D_FF), BF16),
                   jax.ShapeDtypeStruct((D_MODEL, D_FF), BF16),
                   jax.ShapeDtypeStruct((D_FF, D_MODEL), BF16)],
        input_output_aliases={0: 0},
        scratch_shapes=scratch,
        compiler_params=_cparams(("arbitrary", "arbitrary")),
        name="ffn_cast_final" if final else "ffn_cast",
    )(x, mods, g2, wi, wi, wo, fg2)

    return pl.pallas_call(
        functools.partial(kern, tile0=1, cast=False),
        grid=(n_rows // tm - 1, D_FF // tf),
        in_specs=[pl.BlockSpec((tm, D_MODEL), lambda i, f: (i + 1, 0))] + small + [
            pl.BlockSpec((D_MODEL, tf), lambda i, f: (0, f)),
            pl.BlockSpec((D_MODEL, tf), lambda i, f: (0, f)),
            pl.BlockSpec((tf, D_MODEL), lambda i, f: (f, 0)),
            pl.BlockSpec((1, D_MODEL), vec)],
        out_specs=pl.BlockSpec((tm, D_MODEL), lambda i, f: (i + 1, 0)),
        out_shape=jax.ShapeDtypeStruct((n_rows, D_MODEL), F32),
        input_output_aliases={0: 0},
        scratch_shapes=scratch,
        compiler_params=_cparams(("arbitrary", "arbitrary")),
        name="ffn_final" if final else "ffn",
    )(x, mods, g2, wig, wiu, wo_bf, fg2)


def _rope(t, cos, sin):
    lane = lax.broadcasted_iota(jnp.int32, (1, HEAD_DIM), 1)
    first_half = (lane % (2 * ROPE_PAIRS)) < ROPE_PAIRS
    outs = []
    for hd in range(t.shape[1] // HEAD_DIM):
        th = t[:, hd * HEAD_DIM:(hd + 1) * HEAD_DIM]
        partner = jnp.where(first_half, pltpu.roll(th, HEAD_DIM - ROPE_PAIRS, 1),
                            pltpu.roll(th, ROPE_PAIRS, 1))
        outs.append(th * cos + partner * sin)
    return jnp.concatenate(outs, axis=1)


def _inproj_kernel(*refs, tm, cast):
    if cast:
        x_ref, m_ref, g_ref, w_in, cos_ref, sin_ref, o_ref, w_ref, xn_ref = refs
    else:
        x_ref, m_ref, g_ref, w_ref, cos_ref, sin_ref, p0_ref, o_ref, xn_ref = refs
    i = pl.program_id(0)
    j = pl.program_id(1)
    q_blocks = K_OFF // PROJ_TN
    plain_end = GATE_OFF // PROJ_TN

    def project(epilogue):
        for c0 in range(0, PROJ_TN, PROJ_CHUNK):
            cs = slice(c0, c0 + PROJ_CHUNK)
            o_ref[:, cs] = epilogue(_dot(xn_ref[...], w_ref[:, cs]), c0).astype(BF16)

    def rope(acc):
        return _rope(acc, cos_ref[...], sin_ref[...])

    def compute():
        @pl.when(j == 0)
        def _():
            _norm_mod_rows(x_ref, xn_ref, g_ref, m_ref, 3, i * tm, tm)

        if cast:
            w_ref[...] = w_in[...].astype(BF16)

        @pl.when(j < q_blocks)
        def _():
            project(lambda acc, c0: rope(acc) * (HEAD_DIM ** -0.5 * LOG2E))

        @pl.when(j == q_blocks)
        def _():
            project(lambda acc, c0: rope(acc) if c0 < V_OFF - K_OFF else acc)

        @pl.when((j > q_blocks) & (j < plain_end))
        def _():
            project(lambda acc, c0: acc)

        @pl.when(j >= plain_end)
        def _():
            project(lambda acc, c0: jax.nn.sigmoid(acc))

    if cast:
        compute()
    else:
        @pl.when(i == 0)
        def _():
            o_ref[...] = p0_ref[...]

        pl.when(i > 0)(compute)


def _inproj(x, mods, norm_g, w_in, layer, cos, sin):
    tm = PROJ_TM
    n_j = IN_COLS // PROJ_TN
    small = _layer_vector_specs(layer, 1)
    row = lambda width: pl.BlockSpec((tm, width), lambda i, j: (i, 0))
    scratch = [pltpu.VMEM((tm, D_MODEL), BF16)]
    g2 = norm_g.reshape(DEPTH * 3, 1, D_MODEL)

    p0, w_bf = pl.pallas_call(
        functools.partial(_inproj_kernel, tm=tm, cast=True),
        grid=(1, n_j),
        in_specs=[pl.BlockSpec((tm, D_MODEL), lambda i, j: (0, 0), pipeline_mode=pl.Buffered(1))] + small + [
            pl.BlockSpec((None, D_MODEL, PROJ_TN), lambda i, j: (layer, 0, j)),
            row(HEAD_DIM), row(HEAD_DIM)],
        out_specs=[pl.BlockSpec((tm, PROJ_TN), lambda i, j: (0, j)),
                   pl.BlockSpec((D_MODEL, PROJ_TN), lambda i, j: (0, j))],
        out_shape=[jax.ShapeDtypeStruct((tm, IN_COLS), BF16),
                   jax.ShapeDtypeStruct((D_MODEL, IN_COLS), BF16)],
        scratch_shapes=scratch,
        compiler_params=_cparams(("arbitrary", "arbitrary")),
        name="inproj_cast",
    )(x, mods, g2, w_in, cos, sin)

    return pl.pallas_call(
        functools.partial(_inproj_kernel, tm=tm, cast=False),
        grid=(ROWS // tm, n_j),
        in_specs=[row(D_MODEL)] + small + [
            pl.BlockSpec((D_MODEL, PROJ_TN), lambda i, j: (0, jnp.where(i == 0, 0, j))),
            row(HEAD_DIM), row(HEAD_DIM),
            pl.BlockSpec((tm, PROJ_TN), lambda i, j: (0, jnp.where(i == 0, j, n_j - 1)))],
        out_specs=pl.BlockSpec((tm, PROJ_TN), lambda i, j: (i, j)),
        out_shape=jax.ShapeDtypeStruct((ROWS, IN_COLS), BF16),
        scratch_shapes=scratch,
        compiler_params=_cparams(("arbitrary", "arbitrary")),
        name="inproj",
    )(x, mods, g2, w_bf, cos, sin, p0)


def _rope_tables():
    pos = np.arange(SEQ)
    inv = ROPE_BASE ** (-np.arange(ROPE_PAIRS, dtype=np.float64) / ROPE_PAIRS)
    ang_r = (pos // GRID_W)[:, None] * inv
    ang_c = (pos % GRID_W)[:, None] * inv
    cos = np.concatenate([np.cos(ang_r), np.cos(ang_r), np.cos(ang_c), np.cos(ang_c)], axis=1)
    sin = np.concatenate([-np.sin(ang_r), np.sin(ang_r), -np.sin(ang_c), np.sin(ang_c)], axis=1)
    cos = np.concatenate([cos, np.ones((CTX_LEN, HEAD_DIM))], axis=0)
    sin = np.concatenate([sin, np.zeros((CTX_LEN, HEAD_DIM))], axis=0)
    return jnp.asarray(cos, F32), jnp.asarray(sin, F32)


def _attn_kernel(sink_ref, q_ref, kp_ref, kc_ref, kn_ref, vp_ref, vc_ref, vn_ref, kx_ref, vx_ref,
                 *rest, layer):
    n_w = len(MIX_WEIGHT_ROWS)
    w_in, o_ref, w_out = rest[:n_w], rest[n_w], rest[n_w + 1:]
    for src, dst in zip(w_in, w_out):
        dst[...] = src[...].astype(BF16)
    n = pl.program_id(0)
    n_lat = SEQ // BLOCK
    r = lax.broadcasted_iota(jnp.int32, (BLOCK, BLOCK), 0)
    c = lax.broadcasted_iota(jnp.int32, (BLOCK, BLOCK), 1)
    is_lat = n < n_lat
    prev_ok = (c >= r) & (n >= 1) & is_lat
    next_ok = (c <= r) & (n < n_lat - 1)
    n_keys = 3 * BLOCK + CTX_LEN
    ones = jnp.ones((n_keys, HEAD_DIM), BF16)
    contract_last = (((1,), (1,)), ((), ()))
    for h in range(N_KV_HEADS):
        hs = slice(h * HEAD_DIM, (h + 1) * HEAD_DIM)
        qg = jnp.concatenate(
            [q_ref[:, (h * GQA_GROUP + g) * HEAD_DIM:(h * GQA_GROUP + g + 1) * HEAD_DIM]
             for g in range(GQA_GROUP)], axis=0)
        kb = jnp.concatenate([kp_ref[:, hs], kc_ref[:, hs], kn_ref[:, hs], kx_ref[:, hs]], axis=0)
        vb = jnp.concatenate([vp_ref[:, hs], vc_ref[:, hs], vn_ref[:, hs], vx_ref[:, hs]], axis=0)
        vb = jnp.concatenate([vb, ones], axis=1)
        s = lax.dot_general(qg, kb, contract_last, preferred_element_type=F32)
        probs, maxes, sinks = [], [], []
        for g in range(GQA_GROUP):
            sg = s[g * BLOCK:(g + 1) * BLOCK]
            parts = [jnp.where(prev_ok, sg[:, 0:BLOCK], NEG_INF),
                     jnp.where(is_lat, sg[:, BLOCK:2 * BLOCK], NEG_INF),
                     jnp.where(next_ok, sg[:, 2 * BLOCK:3 * BLOCK], NEG_INF),
                     sg[:, 3 * BLOCK:4 * BLOCK], sg[:, 4 * BLOCK:5 * BLOCK]]
            sink = sink_ref[layer * N_Q_HEADS + h * GQA_GROUP + g] * LOG2E
            lane_max = jnp.maximum(jnp.maximum(jnp.maximum(parts[0], parts[1]),
                                               jnp.maximum(parts[2], parts[3])), parts[4])
            m = jnp.maximum(jnp.max(lane_max, axis=-1, keepdims=True), sink)
            probs.append(jnp.concatenate([jnp.exp2(pt - m).astype(BF16) for pt in parts], axis=1))
            maxes.append(m)
            sinks.append(sink)
        oe = _dot(jnp.concatenate(probs, axis=0), vb)
        for g in range(GQA_GROUP):
            hq = h * GQA_GROUP + g
            og = oe[g * BLOCK:(g + 1) * BLOCK]
            denom = og[:, HEAD_DIM:HEAD_DIM + 1] + jnp.exp2(sinks[g] - maxes[g])
            o_ref[:, hq * HEAD_DIM:(hq + 1) * HEAD_DIM] = (og[:, :HEAD_DIM] / denom).astype(BF16)


def _attention(p, sinks, layer, n_rows, mix_weights):
    n_blocks = n_rows // BLOCK
    last = ROWS // BLOCK - 1
    kcol, vcol = K_OFF // 512, V_OFF // 512
    ctx_blk = SEQ // CTX_LEN
    kv = lambda col, off: pl.BlockSpec(
        (BLOCK, 512), lambda n, s: (jnp.clip(n + off, 0, last), col))
    n_slabs = SEQ // BLOCK
    slab = lambda n: jnp.minimum(n, n_slabs - 1)
    w_in_specs, w_out_specs, w_out_shapes = [], [], []
    for w, rows in zip(mix_weights, MIX_WEIGHT_ROWS):
        assert w.shape[1] == rows
        sr, width = rows // n_slabs, w.shape[2]
        w_in_specs.append(pl.BlockSpec((None, sr, width), lambda n, s: (layer, slab(n), 0)))
        w_out_specs.append(pl.BlockSpec((sr, width), lambda n, s: (slab(n), 0)))
        w_out_shapes.append(jax.ShapeDtypeStruct((rows, width), BF16))
    outs = pl.pallas_call(
        functools.partial(_attn_kernel, layer=layer),
        grid_spec=pltpu.PrefetchScalarGridSpec(
            num_scalar_prefetch=1,
            grid=(n_blocks,),
            in_specs=[
                pl.BlockSpec((BLOCK, D_MODEL), lambda n, s: (n, 0)),
                kv(kcol, -1), kv(kcol, 0), kv(kcol, 1),
                kv(vcol, -1), kv(vcol, 0), kv(vcol, 1),
                pl.BlockSpec((CTX_LEN, 512), lambda n, s: (ctx_blk, kcol)),
                pl.BlockSpec((CTX_LEN, 512), lambda n, s: (ctx_blk, vcol)),
            ] + w_in_specs,
            out_specs=[pl.BlockSpec((BLOCK, D_MODEL), lambda n, s: (n, 0))] + w_out_specs,
        ),
        out_shape=[jax.ShapeDtypeStruct((n_rows, D_MODEL), BF16)] + w_out_shapes,
        compiler_params=_cparams(("arbitrary",)),
        name="attention",
    )(sinks, p, p, p, p, p, p, p, p, p, *mix_weights)
    return outs[0], outs[1:]


def _mix_kernel(x_ref, a_ref, u_ref, up_ref, un_ref, cb_ref, cc_ref, ccp_ref, ccn_ref,
                cx_ref, cxp_ref, cxn_ref, g0a_ref, g0b_ref, g1a_ref, g1b_ref, g2a_ref, g2b_ref,
                m_ref, pw_ref, ps_ref, cw_ref, wa_ref, wp_ref, wc_ref, wo_ref, o_ref, ub_ref):
    t = pl.program_id(0)
    tm = MIX_TM
    lat_tiles = SEQ // tm
    is_ctx = t >= lat_tiles
    seg_first = (t == 0) | is_ctx
    seg_last = (t == lat_tiles - 1) | is_ctx
    seg_len = jnp.where(is_ctx, CTX_LEN, SEQ)
    row = lax.broadcasted_iota(jnp.int32, (tm, 1), 0)
    pos = jnp.where(is_ctx, 0, t * tm) + row

    y_attn = _dot(a_ref[...], wa_ref[...])

    ub_ref[0:tm, :] = u_ref[...]
    ub_ref[tm:tm + HALO, :] = jnp.where(seg_last, jnp.zeros_like(un_ref[...]), un_ref[...])
    ub_ref[tm + HALO:POOL_K - HALO, :] = jnp.zeros((POOL_K - 2 * HALO - tm, POOL_WIDTH), BF16)
    ub_ref[POOL_K - HALO:POOL_K, :] = jnp.where(seg_first, jnp.zeros_like(up_ref[...]), up_ref[...])
    br = lax.broadcasted_iota(jnp.int32, (tm, POOL_K), 0)
    bc = lax.broadcasted_iota(jnp.int32, (tm, POOL_K), 1)
    d = bc - br
    d = jnp.where(d >= POOL_K // 2, d - POOL_K, d)
    pooled = []
    for gi, w in enumerate(POOL_WINDOWS):
        lo, hi = w // 2, w - w // 2 - 1
        gs = slice(gi * POOL_GROUP, (gi + 1) * POOL_GROUP)
        band = ((d >= -lo) & (d <= hi)).astype(BF16)
        wsum = _dot(band, ub_ref[:, gs])
        cnt = (jnp.minimum(pos + hi, seg_len - 1) - jnp.maximum(pos - lo, 0) + 1).astype(F32)
        y = wsum / cnt - u_ref[:, gs].astype(F32)
        pooled.append(_dot(y.astype(BF16), pw_ref[gs, :]))
    y_pool = jnp.concatenate(pooled, axis=1) * ps_ref[...]
    y_pool = _dot(y_pool.astype(BF16), wp_ref[...])

    z = cc_ref[...].astype(F32) * cx_ref[...].astype(F32)
    z_prev = ccp_ref[HALO - 1:HALO, :].astype(F32) * cxp_ref[HALO - 1:HALO, :].astype(F32)
    z_next = ccn_ref[0:1, :].astype(F32) * cxn_ref[0:1, :].astype(F32)
    z_prev = jnp.where(seg_first, 0.0, z_prev)
    z_next = jnp.where(seg_last, 0.0, z_next)
    zm1 = jnp.where(row == 0, z_prev, pltpu.roll(z, 1, 0))
    zp1 = jnp.where(row == tm - 1, z_next, pltpu.roll(z, tm - 1, 0))
    conv = zm1 * cw_ref[0:1, :] + z * cw_ref[1:2, :] + zp1 * cw_ref[2:3, :]
    y_conv = _dot((cb_ref[...].astype(F32) * conv).astype(BF16), wc_ref[...])

    halves = []
    for k, (g0, g1, g2) in enumerate(((g0a_ref, g1a_ref, g2a_ref), (g0b_ref, g1b_ref, g2b_ref))):
        cs = slice(k * GATE_TN, (k + 1) * GATE_TN)
        halves.append((g0[...].astype(F32) * y_attn[:, cs] + g1[...].astype(F32) * y_pool[:, cs]
                       + g2[...].astype(F32) * y_conv[:, cs]).astype(BF16))
    out = _dot(jnp.concatenate(halves, axis=1), wo_ref[...])
    o_ref[...] = x_ref[...] + _mod_rows(m_ref, 5, is_ctx) * out


def _mix(x, p, attn, mods, layer, pool_scale, conv_w, weights_bf, n_rows):
    w_attn_out, w_pool_out, w_conv_out, w_o, pool_w = weights_bf
    tm = MIX_TM
    n_tiles = n_rows // tm
    hb = tm // HALO
    last_halo = ROWS // HALO - 1
    tile = lambda width, col: pl.BlockSpec((tm, width), lambda t: (t, col))
    prev = lambda col: pl.BlockSpec((HALO, POOL_WIDTH), lambda t: (jnp.maximum(t * hb - 1, 0), col))
    nxt = lambda col: pl.BlockSpec((HALO, POOL_WIDTH), lambda t: (jnp.minimum((t + 1) * hb, last_halo), col))
    const = lambda shape: pl.BlockSpec(shape, lambda t: (0,) * len(shape), pipeline_mode=pl.Buffered(1))
    ucol, cbcol, cccol, cxcol = POOL_OFF // 1024, CB_OFF // 1024, CC_OFF // 1024, CX_OFF // 1024
    gcol = GATE_OFF // GATE_TN
    return pl.pallas_call(
        _mix_kernel,
        grid=(n_tiles,),
        in_specs=[
            tile(D_MODEL, 0),
            tile(D_MODEL, 0),
            tile(1024, ucol), prev(ucol), nxt(ucol),
            tile(1024, cbcol),
            tile(1024, cccol), prev(cccol), nxt(cccol),
            tile(1024, cxcol), prev(cxcol), nxt(cxcol),
            *[tile(GATE_TN, gcol + k) for k in range(3 * D_MODEL // GATE_TN)],
            pl.BlockSpec((None, 8, N_MODS * D_MODEL), lambda t: (layer, 0, 0), pipeline_mode=pl.Buffered(1)),
            const((POOL_WIDTH, POOL_GROUP)),
            pl.BlockSpec((None, 1, POOL_WIDTH), lambda t: (layer, 0, 0), pipeline_mode=pl.Buffered(1)),
            pl.BlockSpec((None, 3, CONV_WIDTH), lambda t: (layer, 0, 0), pipeline_mode=pl.Buffered(1)),
            const((D_MODEL, D_MODEL)),
            const((POOL_WIDTH, D_MODEL)),
            const((CONV_WIDTH, D_MODEL)),
            const((D_MODEL, D_MODEL)),
        ],
        out_specs=pl.BlockSpec((tm, D_MODEL), lambda t: (t, 0)),
        out_shape=jax.ShapeDtypeStruct((n_rows, D_MODEL), F32),
        scratch_shapes=[pltpu.VMEM((POOL_K, POOL_WIDTH), BF16)],
        compiler_params=_cparams(("arbitrary",)),
        name="mix",
    )(x, attn, *([p] * 16), mods, pool_w, pool_scale.reshape(DEPTH, 1, POOL_WIDTH),
      conv_w, w_attn_out, w_pool_out, w_conv_out, w_o)


def kernel(x, c, ctx, c_ctx, w_ada, b_ada, norm_g, ffn1_wi, ffn1_wo, w_in, attn_sink, pool_w, pool_scale,
           conv_w, w_attn_out, w_pool_out, w_conv_out, w_o, ffn2_wi, ffn2_wo, final_g):
    assert x.shape == (1, SEQ, D_MODEL) and ctx.shape == (1, CTX_LEN, D_MODEL)
    cond = jnp.concatenate([c.reshape(1, D_MODEL), c_ctx.reshape(1, D_MODEL),
                            jnp.zeros((6, D_MODEL), F32)], axis=0)
    mods = _ada(cond, w_ada, b_ada)
    cos, sin = _rope_tables()
    mix_weights = (w_attn_out, w_pool_out, w_conv_out, w_o,
                   pool_w.reshape(DEPTH, POOL_WIDTH, POOL_GROUP))
    sinks = attn_sink.reshape(DEPTH * N_Q_HEADS)

    h = jnp.concatenate([x[0], ctx[0]], axis=0)
    for l in range(DEPTH):
        last = l == DEPTH - 1
        h = _ffn(h, FFN_TM, FFN_TF, mods, norm_g, ffn1_wi, ffn1_wo, l, 0, final_g)
        p = _inproj(h, mods, norm_g, w_in, l, cos, sin)
        n_rows = SEQ if last else ROWS
        attn, mix_bf = _attention(p, sinks, l, n_rows, mix_weights)
        h = _mix(h, p, attn, mods, l, pool_scale, conv_w, mix_bf, n_rows)
        if last:
            h = _ffn(h, FFN_TM_LATENT, FFN_TF, mods, norm_g, ffn2_wi, ffn2_wo, l, 6, final_g, final=True)
        else:
            h = _ffn(h, FFN_TM, FFN_TF, mods, norm_g, ffn2_wi, ffn2_wo, l, 6, final_g)
    return h.reshape(1, SEQ, D_MODEL)
```

```python
import functools

import jax
import jax.numpy as jnp
import numpy as np
from jax import lax
from jax.experimental import pallas as pl
from jax.experimental.pallas import tpu as pltpu

D_MODEL = 2048
SEQ = 8192
CTX_LEN = 256
ROWS = SEQ + CTX_LEN
DEPTH = 2
GRID_W = 64
HEAD_DIM = 128
N_Q_HEADS = 16
N_KV_HEADS = 4
GQA_GROUP = 4
WINDOW = 128
BLOCK = 128
ROPE_PAIRS = 32
ROPE_BASE = 10000.0
POOL_WINDOWS = (2, 4, 8, 16)
POOL_WIDTH = 1024
POOL_GROUP = 256
CONV_WIDTH = 1024
D_FF = 5632
N_MODS = 9
EPS = 1e-6
NEG_INF = -1e30
LOG2E = 1.4426950408889634

K_OFF = 2048
V_OFF = 2560
POOL_OFF = 3072
CB_OFF = 4096
CC_OFF = 5120
CX_OFF = 6144
GATE_OFF = 7168
IN_COLS = GATE_OFF + 3 * D_MODEL

BF16 = jnp.bfloat16
F32 = jnp.float32

VMEM_LIMIT_BYTES = 62 * 1024 * 1024

FFN_TM = 1056
FFN_TM_LATENT = 1024
PROJ_TM = 1408
FFN_TF = 512
HOST_TN = 256
FFN_TF_CAST = 256
PROJ_TN = 1024
PROJ_CHUNK = 512
ROW_CHUNK = 32
ROWS_UNROLL = 4
HIDDEN_CHUNK = 256
COL_CHUNK = 512
MIX_TM = 256
MIX_WEIGHT_ROWS = (2048, 1024, 1024, 2048, 1024)
GATE_TN = 1024
HALO = 16
POOL_K = 384


def _cparams(sem):
    return pltpu.CompilerParams(dimension_semantics=sem, vmem_limit_bytes=VMEM_LIMIT_BYTES)


def _layer_vector_specs(layer, norm_idx):
    once = pl.Buffered(1)
    return [pl.BlockSpec((None, 8, N_MODS * D_MODEL), lambda *_: (layer, 0, 0), pipeline_mode=once),
            pl.BlockSpec((None, 1, D_MODEL), lambda *_: (layer * 3 + norm_idx, 0, 0), pipeline_mode=once)]


def _dot(a, b):
    return jnp.dot(a, b, preferred_element_type=F32)


def _norm_mod(x, g, shift, scale):
    y = x * lax.rsqrt(jnp.mean(x * x, axis=-1, keepdims=True) + EPS)
    return y * (g * (1.0 + scale)) + shift


def _mod_rows(m_ref, k, is_ctx):
    lo, hi = k * D_MODEL, (k + 1) * D_MODEL
    return jnp.where(is_ctx, m_ref[1:2, lo:hi], m_ref[0:1, lo:hi])


assert SEQ % ROW_CHUNK == 0


def _is_ctx_rows(first_row):
    return first_row >= SEQ


def _rows_loop(tm, body, unroll):
    def step(r, carry):
        body(pl.multiple_of(r * ROW_CHUNK, ROW_CHUNK))
        return carry
    lax.fori_loop(0, tm // ROW_CHUNK, step, 0, unroll=unroll)


def _norm_mod_rows(x_ref, xm_ref, g_ref, m_ref, k_shift, row0, tm):
    def rows(r0):
        rs = pl.ds(r0, ROW_CHUNK)
        is_ctx = _is_ctx_rows(row0 + r0)
        xm_ref[rs, :] = _norm_mod(x_ref[rs, :], g_ref[...], _mod_rows(m_ref, k_shift, is_ctx),
                                  _mod_rows(m_ref, k_shift + 1, is_ctx)).astype(BF16)
    _rows_loop(tm, rows, ROWS_UNROLL)


def _ada_kernel(c_ref, w_ref, b_ref, o_ref):
    c = c_ref[...]
    s = c * jax.nn.sigmoid(c)
    o_ref[...] = _dot(s.astype(BF16), w_ref[...].astype(BF16)) + b_ref[...]


def _ada(cond, w_ada, b_ada):
    tn = 1024
    n_cols = N_MODS * D_MODEL
    return pl.pallas_call(
        _ada_kernel,
        grid=(DEPTH, n_cols // tn),
        in_specs=[
            pl.BlockSpec((8, D_MODEL), lambda l, j: (0, 0)),
            pl.BlockSpec((None, D_MODEL, tn), lambda l, j: (l, 0, j)),
            pl.BlockSpec((None, 1, tn), lambda l, j: (l, 0, j)),
        ],
        out_specs=pl.BlockSpec((None, 8, tn), lambda l, j: (l, 0, j)),
        out_shape=jax.ShapeDtypeStruct((DEPTH, 8, n_cols), F32),
        compiler_params=_cparams(("arbitrary", "arbitrary")),
        name="ada",
    )(cond, w_ada, b_ada.reshape(DEPTH, 1, n_cols))


def _ffn_kernel(*refs, tm, tile0, k0, final, cast, host):
    if cast:
        (x_ref, m_ref, g_ref, wig_in, wiu_in, wo_in, fg_ref,
         o_ref, wig_ref, wiu_ref, wo_ref, xm_ref) = refs
    elif host:
        x_ref, m_ref, g_ref, wig_ref, wiu_ref, wo_ref, fg_ref, next_in, o_ref, next_out, xm_ref = refs
        next_out[...] = next_in[...].astype(BF16)
    else:
        x_ref, m_ref, g_ref, wig_ref, wiu_ref, wo_ref, fg_ref, o_ref, xm_ref = refs
    row0 = (pl.program_id(0) + tile0) * tm
    f = pl.program_id(1)

    @pl.when(f == 0)
    def _():
        _norm_mod_rows(x_ref, xm_ref, g_ref, m_ref, k0, row0, tm)
        o_ref[...] = jnp.zeros_like(o_ref)

    if cast:
        wig_ref[...] = wig_in[...].astype(BF16)
        wiu_ref[...] = wiu_in[...].astype(BF16)
        wo_ref[...] = wo_in[...].astype(BF16)
    xm = xm_ref[...]
    parts = []
    for h0 in range(0, wig_ref.shape[1], HIDDEN_CHUNK):
        hs = slice(h0, h0 + HIDDEN_CHUNK)
        g = _dot(xm, wig_ref[:, hs])
        u = _dot(xm, wiu_ref[:, hs])
        parts.append(((g * jax.nn.sigmoid(g)) * u).astype(BF16))
    a = parts[0] if len(parts) == 1 else jnp.concatenate(parts, axis=1)
    for c0 in range(0, D_MODEL, COL_CHUNK):
        o_ref[:, c0:c0 + COL_CHUNK] += _dot(a, wo_ref[:, c0:c0 + COL_CHUNK])

    @pl.when(f == pl.num_programs(1) - 1)
    def _():
        def rows(r0):
            rs = pl.ds(r0, ROW_CHUNK)
            gate = _mod_rows(m_ref, k0 + 2, _is_ctx_rows(row0 + r0))
            y = x_ref[rs, :] + (0.5 * gate) * o_ref[rs, :]
            if final:
                y = (y * lax.rsqrt(jnp.mean(y * y, axis=-1, keepdims=True) + EPS)) * fg_ref[...]
            o_ref[rs, :] = y
        _rows_loop(tm, rows, ROWS_UNROLL if final else 1)


def _ffn(x, tm, tf, mods, norm_g, wi, wo, layer, k0, final_g, final=False, next_w=None):
    n_rows = x.shape[0]
    kern = functools.partial(_ffn_kernel, tm=tm, k0=k0, final=final)
    vec = lambda i, f: (0, 0)
    small = _layer_vector_specs(layer, k0 // 3)
    scratch = [pltpu.VMEM((tm, D_MODEL), BF16)]
    g2, fg2 = norm_g.reshape(DEPTH * 3, 1, D_MODEL), final_g.reshape(1, D_MODEL)

    tc = FFN_TF_CAST
    n_fc = D_FF // tc
    x, wig, wiu, wo_bf = pl.pallas_call(
        functools.partial(kern, tile0=0, cast=True, host=False),
        grid=(1, n_fc),
        in_specs=[pl.BlockSpec((tm, D_MODEL), lambda i, f: (0, 0), pipeline_mode=pl.Buffered(1))] + small + [
            pl.BlockSpec((None, D_MODEL, tc), lambda i, f: (layer, 0, f)),
            pl.BlockSpec((None, D_MODEL, tc), lambda i, f: (layer, 0, n_fc + f)),
            pl.BlockSpec((None, tc, D_MODEL), lambda i, f: (layer, f, 0)),
            pl.BlockSpec((1, D_MODEL), vec)],
        out_specs=[pl.BlockSpec((tm, D_MODEL), lambda i, f: (0, 0)),
                   pl.BlockSpec((D_MODEL, tc), lambda i, f: (0, f)),
                   pl.BlockSpec((D_MODEL, tc), lambda i, f: (0, f)),
                   pl.BlockSpec((tc, D_MODEL), lambda i, f: (f, 0))],
        out_shape=[jax.ShapeDtypeStruct((n_rows, D_MODEL), F32),
                   jax.ShapeDtypeStruct((D_MODEL, D_FF), BF16),
                   jax.ShapeDtypeStruct((D_MODEL, D_FF), BF16),
                   jax.ShapeDtypeStruct((D_FF, D_MODEL), BF16)],
        input_output_aliases={0: 0},
        scratch_shapes=scratch,
        compiler_params=_cparams(("arbitrary", "arbitrary")),
        name="ffn_cast_final" if final else "ffn_cast",
    )(x, mods, g2, wi, wi, wo, fg2)

    n_i, n_f = n_rows // tm - 1, D_FF // tf
    in_specs = [pl.BlockSpec((tm, D_MODEL), lambda i, f: (i + 1, 0))] + small + [
        pl.BlockSpec((D_MODEL, tf), lambda i, f: (0, f)),
        pl.BlockSpec((D_MODEL, tf), lambda i, f: (0, f)),
        pl.BlockSpec((tf, D_MODEL), lambda i, f: (f, 0)),
        pl.BlockSpec((1, D_MODEL), vec)]
    out_specs = [pl.BlockSpec((tm, D_MODEL), lambda i, f: (i + 1, 0))]
    out_shape = [jax.ShapeDtypeStruct((n_rows, D_MODEL), F32)]
    operands = [x, mods, g2, wig, wiu, wo_bf, fg2]
    if next_w is not None:
        cols = next_w.shape[2]
        n_slabs = cols // HOST_TN
        assert n_slabs <= n_i * n_f
        slab = lambda i, f: jnp.minimum(i * n_f + f, n_slabs - 1)
        in_specs.append(pl.BlockSpec((None, D_MODEL, HOST_TN), lambda i, f: (layer, 0, slab(i, f))))
        out_specs.append(pl.BlockSpec((D_MODEL, HOST_TN), lambda i, f: (0, slab(i, f))))
        out_shape.append(jax.ShapeDtypeStruct((D_MODEL, cols), BF16))
        operands.append(next_w)
    outs = pl.pallas_call(
        functools.partial(kern, tile0=1, cast=False, host=next_w is not None),
        grid=(n_i, n_f),
        in_specs=in_specs,
        out_specs=out_specs,
        out_shape=out_shape,
        input_output_aliases={0: 0},
        scratch_shapes=scratch,
        compiler_params=_cparams(("arbitrary", "arbitrary")),
        name="ffn_final" if final else "ffn",
    )(*operands)
    return outs if next_w is not None else outs[0]


def _rope(t, cos, sin):
    lane = lax.broadcasted_iota(jnp.int32, (1, HEAD_DIM), 1)
    first_half = (lane % (2 * ROPE_PAIRS)) < ROPE_PAIRS
    outs = []
    for hd in range(t.shape[1] // HEAD_DIM):
        th = t[:, hd * HEAD_DIM:(hd + 1) * HEAD_DIM]
        partner = jnp.where(first_half, pltpu.roll(th, HEAD_DIM - ROPE_PAIRS, 1),
                            pltpu.roll(th, ROPE_PAIRS, 1))
        outs.append(th * cos + partner * sin)
    return jnp.concatenate(outs, axis=1)


def _inproj_kernel(x_ref, m_ref, g_ref, w_ref, cos_ref, sin_ref, o_ref, xn_ref, *, tm):
    j = pl.program_id(1)
    q_blocks = K_OFF // PROJ_TN
    plain_end = GATE_OFF // PROJ_TN

    def project(epilogue):
        for c0 in range(0, PROJ_TN, PROJ_CHUNK):
            cs = slice(c0, c0 + PROJ_CHUNK)
            o_ref[:, cs] = epilogue(_dot(xn_ref[...], w_ref[:, cs]), c0).astype(BF16)

    def rope(acc):
        return _rope(acc, cos_ref[...], sin_ref[...])

    @pl.when(j == 0)
    def _():
        _norm_mod_rows(x_ref, xn_ref, g_ref, m_ref, 3, pl.program_id(0) * tm, tm)

    @pl.when(j < q_blocks)
    def _():
        project(lambda acc, c0: rope(acc) * (HEAD_DIM ** -0.5 * LOG2E))

    @pl.when(j == q_blocks)
    def _():
        project(lambda acc, c0: rope(acc) if c0 < V_OFF - K_OFF else acc)

    @pl.when((j > q_blocks) & (j < plain_end))
    def _():
        project(lambda acc, c0: acc)

    @pl.when(j >= plain_end)
    def _():
        project(lambda acc, c0: 0.5 * jnp.tanh(0.5 * acc) + 0.5)


def _inproj(x, mods, norm_g, w_bf, layer, cos, sin):
    tm = PROJ_TM
    row = lambda width: pl.BlockSpec((tm, width), lambda i, j: (i, 0))
    return pl.pallas_call(
        functools.partial(_inproj_kernel, tm=tm),
        grid=(ROWS // tm, IN_COLS // PROJ_TN),
        in_specs=[row(D_MODEL)] + _layer_vector_specs(layer, 1) + [
            pl.BlockSpec((D_MODEL, PROJ_TN), lambda i, j: (0, j)),
            row(HEAD_DIM), row(HEAD_DIM)],
        out_specs=pl.BlockSpec((tm, PROJ_TN), lambda i, j: (i, j)),
        out_shape=jax.ShapeDtypeStruct((ROWS, IN_COLS), BF16),
        scratch_shapes=[pltpu.VMEM((tm, D_MODEL), BF16)],
        compiler_params=_cparams(("arbitrary", "arbitrary")),
        name="inproj",
    )(x, mods, norm_g.reshape(DEPTH * 3, 1, D_MODEL), w_bf, cos, sin)


def _rope_tables():
    pos = np.arange(SEQ)
    inv = ROPE_BASE ** (-np.arange(ROPE_PAIRS, dtype=np.float64) / ROPE_PAIRS)
    ang_r = (pos // GRID_W)[:, None] * inv
    ang_c = (pos % GRID_W)[:, None] * inv
    cos = np.concatenate([np.cos(ang_r), np.cos(ang_r), np.cos(ang_c), np.cos(ang_c)], axis=1)
    sin = np.concatenate([-np.sin(ang_r), np.sin(ang_r), -np.sin(ang_c), np.sin(ang_c)], axis=1)
    cos = np.concatenate([cos, np.ones((CTX_LEN, HEAD_DIM))], axis=0)
    sin = np.concatenate([sin, np.zeros((CTX_LEN, HEAD_DIM))], axis=0)
    return jnp.asarray(cos, F32), jnp.asarray(sin, F32)


def _attn_kernel(sink_ref, q_ref, kp_ref, kc_ref, kn_ref, vp_ref, vc_ref, vn_ref, kx_ref, vx_ref,
                 *rest, layer):
    n_w = len(MIX_WEIGHT_ROWS)
    w_in, o_ref, w_out = rest[:n_w], rest[n_w], rest[n_w + 1:]
    for src, dst in zip(w_in, w_out):
        dst[...] = src[...].astype(BF16)
    n = pl.program_id(0)
    n_lat = SEQ // BLOCK
    r = lax.broadcasted_iota(jnp.int32, (BLOCK, BLOCK), 0)
    c = lax.broadcasted_iota(jnp.int32, (BLOCK, BLOCK), 1)
    is_lat = n < n_lat
    prev_ok = (c >= r) & (n >= 1) & is_lat
    next_ok = (c <= r) & (n < n_lat - 1)
    n_keys = 3 * BLOCK + CTX_LEN
    ones = jnp.ones((n_keys, HEAD_DIM), BF16)
    contract_last = (((1,), (1,)), ((), ()))
    for h in range(N_KV_HEADS):
        hs = slice(h * HEAD_DIM, (h + 1) * HEAD_DIM)
        qg = jnp.concatenate(
            [q_ref[:, (h * GQA_GROUP + g) * HEAD_DIM:(h * GQA_GROUP + g + 1) * HEAD_DIM]
             for g in range(GQA_GROUP)], axis=0)
        kb = jnp.concatenate([kp_ref[:, hs], kc_ref[:, hs], kn_ref[:, hs], kx_ref[:, hs]], axis=0)
        vb = jnp.concatenate([vp_ref[:, hs], vc_ref[:, hs], vn_ref[:, hs], vx_ref[:, hs]], axis=0)
        vb = jnp.concatenate([vb, ones], axis=1)
        s = lax.dot_general(qg, kb, contract_last, preferred_element_type=F32)
        probs, maxes, sinks = [], [], []
        for g in range(GQA_GROUP):
            sg = s[g * BLOCK:(g + 1) * BLOCK]
            parts = [jnp.where(prev_ok, sg[:, 0:BLOCK], NEG_INF),
                     jnp.where(is_lat, sg[:, BLOCK:2 * BLOCK], NEG_INF),
                     jnp.where(next_ok, sg[:, 2 * BLOCK:3 * BLOCK], NEG_INF),
                     sg[:, 3 * BLOCK:4 * BLOCK], sg[:, 4 * BLOCK:5 * BLOCK]]
            sink = sink_ref[layer * N_Q_HEADS + h * GQA_GROUP + g] * LOG2E
            lane_max = jnp.maximum(jnp.maximum(jnp.maximum(parts[0], parts[1]),
                                               jnp.maximum(parts[2], parts[3])), parts[4])
            m = jnp.maximum(jnp.max(lane_max, axis=-1, keepdims=True), sink)
            probs.append(jnp.concatenate([jnp.exp2(pt - m).astype(BF16) for pt in parts], axis=1))
            maxes.append(m)
            sinks.append(sink)
        oe = _dot(jnp.concatenate(probs, axis=0), vb)
        for g in range(GQA_GROUP):
            hq = h * GQA_GROUP + g
            og = oe[g * BLOCK:(g + 1) * BLOCK]
            denom = og[:, HEAD_DIM:HEAD_DIM + 1] + jnp.exp2(sinks[g] - maxes[g])
            o_ref[:, hq * HEAD_DIM:(hq + 1) * HEAD_DIM] = (og[:, :HEAD_DIM] / denom).astype(BF16)


def _attention(p, sinks, layer, n_rows, mix_weights):
    n_blocks = n_rows // BLOCK
    last = ROWS // BLOCK - 1
    kcol, vcol = K_OFF // 512, V_OFF // 512
    ctx_blk = SEQ // CTX_LEN
    kv = lambda col, off: pl.BlockSpec(
        (BLOCK, 512), lambda n, *_: (jnp.clip(n + off, 0, last), col))
    n_slabs = SEQ // BLOCK
    slab = lambda n: jnp.minimum(n, n_slabs - 1)
    w_in_specs, w_out_specs, w_out_shapes = [], [], []
    for w, rows in zip(mix_weights, MIX_WEIGHT_ROWS):
        assert w.shape[1] == rows
        sr, width = rows // n_slabs, w.shape[2]
        w_in_specs.append(pl.BlockSpec((None, sr, width), lambda n, *_: (layer, slab(n), 0)))
        w_out_specs.append(pl.BlockSpec((sr, width), lambda n, *_: (slab(n), 0)))
        w_out_shapes.append(jax.ShapeDtypeStruct((rows, width), BF16))
    outs = pl.pallas_call(
        functools.partial(_attn_kernel, layer=layer),
        grid_spec=pltpu.PrefetchScalarGridSpec(
            num_scalar_prefetch=1,
            grid=(n_blocks,),
            in_specs=[
                pl.BlockSpec((BLOCK, D_MODEL), lambda n, *_: (n, 0)),
                kv(kcol, -1), kv(kcol, 0), kv(kcol, 1),
                kv(vcol, -1), kv(vcol, 0), kv(vcol, 1),
                pl.BlockSpec((CTX_LEN, 512), lambda n, *_: (ctx_blk, kcol)),
                pl.BlockSpec((CTX_LEN, 512), lambda n, *_: (ctx_blk, vcol)),
            ] + w_in_specs,
            out_specs=[pl.BlockSpec((BLOCK, D_MODEL), lambda n, *_: (n, 0))] + w_out_specs,
        ),
        out_shape=[jax.ShapeDtypeStruct((n_rows, D_MODEL), BF16)] + w_out_shapes,
        compiler_params=_cparams(("arbitrary",)),
        name="attention",
    )(sinks, p, p, p, p, p, p, p, p, p, *mix_weights)
    return outs[0], outs[1:]


def _mix_kernel(x_ref, a_ref, u_ref, up_ref, un_ref, cb_ref, cc_ref, ccp_ref, ccn_ref,
                cx_ref, cxp_ref, cxn_ref, g0a_ref, g0b_ref, g1a_ref, g1b_ref, g2a_ref, g2b_ref,
                m_ref, pw_ref, ps_ref, cw_ref, wa_ref, wp_ref, wc_ref, wo_ref, o_ref, ub_ref):
    t = pl.program_id(0)
    tm = MIX_TM
    lat_tiles = SEQ // tm
    is_ctx = t >= lat_tiles
    seg_first = (t == 0) | is_ctx
    seg_last = (t == lat_tiles - 1) | is_ctx
    seg_len = jnp.where(is_ctx, CTX_LEN, SEQ)
    row = lax.broadcasted_iota(jnp.int32, (tm, 1), 0)
    pos = jnp.where(is_ctx, 0, t * tm) + row

    y_attn = _dot(a_ref[...], wa_ref[...])

    ub_ref[0:tm, :] = u_ref[...]
    ub_ref[tm:tm + HALO, :] = jnp.where(seg_last, jnp.zeros_like(un_ref[...]), un_ref[...])
    ub_ref[tm + HALO:POOL_K - HALO, :] = jnp.zeros((POOL_K - 2 * HALO - tm, POOL_WIDTH), BF16)
    ub_ref[POOL_K - HALO:POOL_K, :] = jnp.where(seg_first, jnp.zeros_like(up_ref[...]), up_ref[...])
    br = lax.broadcasted_iota(jnp.int32, (tm, POOL_K), 0)
    bc = lax.broadcasted_iota(jnp.int32, (tm, POOL_K), 1)
    d = bc - br
    d = jnp.where(d >= POOL_K // 2, d - POOL_K, d)
    pooled = []
    for gi, w in enumerate(POOL_WINDOWS):
        lo, hi = w // 2, w - w // 2 - 1
        gs = slice(gi * POOL_GROUP, (gi + 1) * POOL_GROUP)
        band = ((d >= -lo) & (d <= hi)).astype(BF16)
        wsum = _dot(band, ub_ref[:, gs])
        cnt = (jnp.minimum(pos + hi, seg_len - 1) - jnp.maximum(pos - lo, 0) + 1).astype(F32)
        y = wsum / cnt - u_ref[:, gs].astype(F32)
        pooled.append(_dot(y.astype(BF16), pw_ref[gs, :]))
    y_pool = jnp.concatenate(pooled, axis=1) * ps_ref[...]
    y_pool = _dot(y_pool.astype(BF16), wp_ref[...])

    z = cc_ref[...].astype(F32) * cx_ref[...].astype(F32)
    z_prev = ccp_ref[HALO - 1:HALO, :].astype(F32) * cxp_ref[HALO - 1:HALO, :].astype(F32)
    z_next = ccn_ref[0:1, :].astype(F32) * cxn_ref[0:1, :].astype(F32)
    z_prev = jnp.where(seg_first, 0.0, z_prev)
    z_next = jnp.where(seg_last, 0.0, z_next)
    zm1 = jnp.where(row == 0, z_prev, pltpu.roll(z, 1, 0))
    zp1 = jnp.where(row == tm - 1, z_next, pltpu.roll(z, tm - 1, 0))
    conv = zm1 * cw_ref[0:1, :] + z * cw_ref[1:2, :] + zp1 * cw_ref[2:3, :]
    y_conv = _dot((cb_ref[...].astype(F32) * conv).astype(BF16), wc_ref[...])

    halves = []
    for k, (g0, g1, g2) in enumerate(((g0a_ref, g1a_ref, g2a_ref), (g0b_ref, g1b_ref, g2b_ref))):
        cs = slice(k * GATE_TN, (k + 1) * GATE_TN)
        halves.append((g0[...].astype(F32) * y_attn[:, cs] + g1[...].astype(F32) * y_pool[:, cs]
                       + g2[...].astype(F32) * y_conv[:, cs]).astype(BF16))
    out = _dot(jnp.concatenate(halves, axis=1), wo_ref[...])
    o_ref[...] = x_ref[...] + _mod_rows(m_ref, 5, is_ctx) * out


def _mix(x, p, attn, mods, layer, pool_scale, conv_w, weights_bf, n_rows):
    w_attn_out, w_pool_out, w_conv_out, w_o, pool_w = weights_bf
    tm = MIX_TM
    n_tiles = n_rows // tm
    hb = tm // HALO
    last_halo = ROWS // HALO - 1
    tile = lambda width, col: pl.BlockSpec((tm, width), lambda t: (t, col))
    prev = lambda col: pl.BlockSpec((HALO, POOL_WIDTH), lambda t: (jnp.maximum(t * hb - 1, 0), col))
    nxt = lambda col: pl.BlockSpec((HALO, POOL_WIDTH), lambda t: (jnp.minimum((t + 1) * hb, last_halo), col))
    const = lambda shape: pl.BlockSpec(shape, lambda t: (0,) * len(shape), pipeline_mode=pl.Buffered(1))
    ucol, cbcol, cccol, cxcol = POOL_OFF // 1024, CB_OFF // 1024, CC_OFF // 1024, CX_OFF // 1024
    gcol = GATE_OFF // GATE_TN
    return pl.pallas_call(
        _mix_kernel,
        grid=(n_tiles,),
        in_specs=[
            tile(D_MODEL, 0),
            tile(D_MODEL, 0),
            tile(1024, ucol), prev(ucol), nxt(ucol),
            tile(1024, cbcol),
            tile(1024, cccol), prev(cccol), nxt(cccol),
            tile(1024, cxcol), prev(cxcol), nxt(cxcol),
            *[tile(GATE_TN, gcol + k) for k in range(3 * D_MODEL // GATE_TN)],
            pl.BlockSpec((None, 8, N_MODS * D_MODEL), lambda t: (layer, 0, 0), pipeline_mode=pl.Buffered(1)),
            const((POOL_WIDTH, POOL_GROUP)),
            pl.BlockSpec((None, 1, POOL_WIDTH), lambda t: (layer, 0, 0), pipeline_mode=pl.Buffered(1)),
            pl.BlockSpec((None, 3, CONV_WIDTH), lambda t: (layer, 0, 0), pipeline_mode=pl.Buffered(1)),
            const((D_MODEL, D_MODEL)),
            const((POOL_WIDTH, D_MODEL)),
            const((CONV_WIDTH, D_MODEL)),
            const((D_MODEL, D_MODEL)),
        ],
        out_specs=pl.BlockSpec((tm, D_MODEL), lambda t: (t, 0)),
        out_shape=jax.ShapeDtypeStruct((n_rows, D_MODEL), F32),
        scratch_shapes=[pltpu.VMEM((POOL_K, POOL_WIDTH), BF16)],
        compiler_params=_cparams(("arbitrary",)),
        name="mix",
    )(x, attn, *([p] * 16), mods, pool_w, pool_scale.reshape(DEPTH, 1, POOL_WIDTH),
      conv_w, w_attn_out, w_pool_out, w_conv_out, w_o)


def kernel(x, c, ctx, c_ctx, w_ada, b_ada, norm_g, ffn1_wi, ffn1_wo, w_in, attn_sink, pool_w, pool_scale,
           conv_w, w_attn_out, w_pool_out, w_conv_out, w_o, ffn2_wi, ffn2_wo, final_g):
    assert x.shape == (1, SEQ, D_MODEL) and ctx.shape == (1, CTX_LEN, D_MODEL)
    cond = jnp.concatenate([c.reshape(1, D_MODEL), c_ctx.reshape(1, D_MODEL),
                            jnp.zeros((6, D_MODEL), F32)], axis=0)
    mods = _ada(cond, w_ada, b_ada)
    cos, sin = _rope_tables()
    mix_weights = (w_attn_out, w_pool_out, w_conv_out, w_o,
                   pool_w.reshape(DEPTH, POOL_WIDTH, POOL_GROUP))
    sinks = attn_sink.reshape(DEPTH * N_Q_HEADS)

    h = jnp.concatenate([x[0], ctx[0]], axis=0)
    for l in range(DEPTH):
        last = l == DEPTH - 1
        h, w_in_bf = _ffn(h, FFN_TM, FFN_TF, mods, norm_g, ffn1_wi, ffn1_wo, l, 0, final_g, next_w=w_in)
        p = _inproj(h, mods, norm_g, w_in_bf, l, cos, sin)
        n_rows = SEQ if last else ROWS
        attn, mix_bf = _attention(p, sinks, l, n_rows, mix_weights)
        h = _mix(h, p, attn, mods, l, pool_scale, conv_w, mix_bf, n_rows)
        if last:
            h = _ffn(h, FFN_TM_LATENT, FFN_TF, mods, norm_g, ffn2_wi, ffn2_wo, l, 6, final_g, final=True)
        else:
            h = _ffn(h, FFN_TM, FFN_TF, mods, norm_g, ffn2_wi, ffn2_wo, l, 6, final_g)
    return h.reshape(1, SEQ, D_MODEL)
```

```python
import functools

import jax
import jax.numpy as jnp
import numpy as np
from jax import lax
from jax.experimental import pallas as pl
from jax.experimental.pallas import tpu as pltpu

D_MODEL = 2048
SEQ = 8192
CTX_LEN = 256
ROWS = SEQ + CTX_LEN
DEPTH = 2
GRID_W = 64
HEAD_DIM = 128
N_Q_HEADS = 16
N_KV_HEADS = 4
GQA_GROUP = 4
WINDOW = 128
BLOCK = 128
ROPE_PAIRS = 32
ROPE_BASE = 10000.0
POOL_WINDOWS = (2, 4, 8, 16)
POOL_WIDTH = 1024
POOL_GROUP = 256
CONV_WIDTH = 1024
D_FF = 5632
N_MODS = 9
EPS = 1e-6
NEG_INF = -1e30
LOG2E = 1.4426950408889634

K_OFF = 2048
V_OFF = 2560
POOL_OFF = 3072
CB_OFF = 4096
CC_OFF = 5120
CX_OFF = 6144
GATE_OFF = 7168
IN_COLS = GATE_OFF + 3 * D_MODEL

BF16 = jnp.bfloat16
F32 = jnp.float32

VMEM_LIMIT_BYTES = 62 * 1024 * 1024

FFN_TM = 1056
FFN_TM_LATENT = 1024
PROJ_TM = 1408
FFN_TF = 512
HOST_TN = 256
FFN_TF_CAST = 256
PROJ_TN = 1024
PROJ_CHUNK = 512
ROW_CHUNK = 32
ROWS_UNROLL = 4
HIDDEN_CHUNK = 256
COL_CHUNK = 512
MIX_TM = 256
MIX_WEIGHT_ROWS = (2048, 1024, 1024, 2048, 1024)
GATE_TN = 1024
HALO = 16
POOL_K = 384


def _cparams(sem):
    return pltpu.CompilerParams(dimension_semantics=sem, vmem_limit_bytes=VMEM_LIMIT_BYTES)


def _layer_vector_specs(layer, norm_idx):
    once = pl.Buffered(1)
    return [pl.BlockSpec((None, 8, N_MODS * D_MODEL), lambda *_: (layer, 0, 0), pipeline_mode=once),
            pl.BlockSpec((None, 1, D_MODEL), lambda *_: (layer * 3 + norm_idx, 0, 0), pipeline_mode=once)]


def _hosted_cast(w, layer, axis, size, step, n_steps):
    rows, cols = w.shape[1:]
    n_slabs = (rows, cols)[axis] // size
    assert n_slabs * size == (rows, cols)[axis] and n_slabs <= n_steps
    block = (size, cols) if axis == 0 else (rows, size)

    def index(*ids):
        k = jnp.minimum(step(*ids), n_slabs - 1)
        return (k, 0) if axis == 0 else (0, k)

    return (pl.BlockSpec((None,) + block, lambda *ids: (layer,) + index(*ids)),
            pl.BlockSpec(block, index),
            jax.ShapeDtypeStruct((rows, cols), BF16))


def _dot(a, b):
    return jnp.dot(a, b, preferred_element_type=F32)


def _norm_mod(x, g, shift, scale):
    y = x * lax.rsqrt(jnp.mean(x * x, axis=-1, keepdims=True) + EPS)
    return y * (g * (1.0 + scale)) + shift


def _mod_rows(m_ref, k, is_ctx):
    lo, hi = k * D_MODEL, (k + 1) * D_MODEL
    return jnp.where(is_ctx, m_ref[1:2, lo:hi], m_ref[0:1, lo:hi])


assert SEQ % ROW_CHUNK == 0


def _is_ctx_rows(first_row):
    return first_row >= SEQ


def _rows_loop(tm, body, unroll):
    def step(r, carry):
        body(pl.multiple_of(r * ROW_CHUNK, ROW_CHUNK))
        return carry
    lax.fori_loop(0, tm // ROW_CHUNK, step, 0, unroll=unroll)


def _norm_mod_rows(x_ref, xm_ref, g_ref, m_ref, k_shift, row0, tm):
    def rows(r0):
        rs = pl.ds(r0, ROW_CHUNK)
        is_ctx = _is_ctx_rows(row0 + r0)
        xm_ref[rs, :] = _norm_mod(x_ref[rs, :], g_ref[...], _mod_rows(m_ref, k_shift, is_ctx),
                                  _mod_rows(m_ref, k_shift + 1, is_ctx)).astype(BF16)
    _rows_loop(tm, rows, ROWS_UNROLL)


def _ada_kernel(c_ref, w_ref, b_ref, o_ref):
    c = c_ref[...]
    s = c * jax.nn.sigmoid(c)
    o_ref[...] = _dot(s.astype(BF16), w_ref[...].astype(BF16)) + b_ref[...]


def _ada(cond, w_ada, b_ada):
    tn = 1024
    n_cols = N_MODS * D_MODEL
    return pl.pallas_call(
        _ada_kernel,
        grid=(DEPTH, n_cols // tn),
        in_specs=[
            pl.BlockSpec((8, D_MODEL), lambda l, j: (0, 0)),
            pl.BlockSpec((None, D_MODEL, tn), lambda l, j: (l, 0, j)),
            pl.BlockSpec((None, 1, tn), lambda l, j: (l, 0, j)),
        ],
        out_specs=pl.BlockSpec((None, 8, tn), lambda l, j: (l, 0, j)),
        out_shape=jax.ShapeDtypeStruct((DEPTH, 8, n_cols), F32),
        compiler_params=_cparams(("arbitrary", "arbitrary")),
        name="ada",
    )(cond, w_ada, b_ada.reshape(DEPTH, 1, n_cols))


def _ffn_kernel(*refs, tm, tile0, k0, final, cast, host):
    if cast:
        (x_ref, m_ref, g_ref, wig_in, wiu_in, wo_in, fg_ref,
         o_ref, wig_ref, wiu_ref, wo_ref, xm_ref) = refs
    else:
        x_ref, m_ref, g_ref, wig_ref, wiu_ref, wo_ref, fg_ref = refs[:7]
        o_ref, xm_ref = refs[7 + host], refs[-1]
        for src, dst in zip(refs[7:7 + host], refs[8 + host:-1]):
            dst[...] = src[...].astype(BF16)
    row0 = (pl.program_id(0) + tile0) * tm
    f = pl.program_id(1)

    @pl.when(f == 0)
    def _():
        _norm_mod_rows(x_ref, xm_ref, g_ref, m_ref, k0, row0, tm)
        o_ref[...] = jnp.zeros_like(o_ref)

    if cast:
        wig_ref[...] = wig_in[...].astype(BF16)
        wiu_ref[...] = wiu_in[...].astype(BF16)
        wo_ref[...] = wo_in[...].astype(BF16)
    xm = xm_ref[...]
    parts = []
    for h0 in range(0, wig_ref.shape[1], HIDDEN_CHUNK):
        hs = slice(h0, h0 + HIDDEN_CHUNK)
        g = _dot(xm, wig_ref[:, hs])
        u = _dot(xm, wiu_ref[:, hs])
        parts.append(((g * jax.nn.sigmoid(g)) * u).astype(BF16))
    a = parts[0] if len(parts) == 1 else jnp.concatenate(parts, axis=1)
    for c0 in range(0, D_MODEL, COL_CHUNK):
        o_ref[:, c0:c0 + COL_CHUNK] += _dot(a, wo_ref[:, c0:c0 + COL_CHUNK])

    @pl.when(f == pl.num_programs(1) - 1)
    def _():
        def rows(r0):
            rs = pl.ds(r0, ROW_CHUNK)
            gate = _mod_rows(m_ref, k0 + 2, _is_ctx_rows(row0 + r0))
            y = x_ref[rs, :] + (0.5 * gate) * o_ref[rs, :]
            if final:
                y = (y * lax.rsqrt(jnp.mean(y * y, axis=-1, keepdims=True) + EPS)) * fg_ref[...]
            o_ref[rs, :] = y
        _rows_loop(tm, rows, ROWS_UNROLL if final else 1)


def _ffn(x, tm, tf, mods, norm_g, wi, wo, layer, k0, final_g, final=False, next_w=None, weights_bf=None):
    n_rows = x.shape[0]
    kern = functools.partial(_ffn_kernel, tm=tm, k0=k0, final=final)
    vec = lambda i, f: (0, 0)
    small = _layer_vector_specs(layer, k0 // 3)
    scratch = [pltpu.VMEM((tm, D_MODEL), BF16)]
    g2, fg2 = norm_g.reshape(DEPTH * 3, 1, D_MODEL), final_g.reshape(1, D_MODEL)
    n_f = D_FF // tf
    if weights_bf is not None:
        wig = wiu = weights_bf[0]
        wo_bf, up_block0, tile0 = weights_bf[1], n_f, 0
    else:
        x, wig, wiu, wo_bf = _ffn_cast_tile(x, kern, small, scratch, mods, g2, fg2, wi, wo, layer, tm, final)
        up_block0, tile0 = 0, 1

    n_i = n_rows // tm - tile0
    in_specs = [pl.BlockSpec((tm, D_MODEL), lambda i, f: (i + tile0, 0))] + small + [
        pl.BlockSpec((D_MODEL, tf), lambda i, f: (0, f)),
        pl.BlockSpec((D_MODEL, tf), lambda i, f: (0, up_block0 + f)),
        pl.BlockSpec((tf, D_MODEL), lambda i, f: (f, 0)),
        pl.BlockSpec((1, D_MODEL), vec)]
    out_specs = [pl.BlockSpec((tm, D_MODEL), lambda i, f: (i + tile0, 0))]
    out_shape = [jax.ShapeDtypeStruct((n_rows, D_MODEL), F32)]
    operands = [x, mods, g2, wig, wiu, wo_bf, fg2]
    if next_w is not None:
        spec_in, spec_out, shape = _hosted_cast(next_w, layer, 1, HOST_TN, lambda i, f: i * n_f + f, n_i * n_f)
        in_specs.append(spec_in)
        out_specs.append(spec_out)
        out_shape.append(shape)
        operands.append(next_w)
    outs = pl.pallas_call(
        functools.partial(kern, tile0=tile0, cast=False, host=int(next_w is not None)),
        grid=(n_i, n_f),
        in_specs=in_specs,
        out_specs=out_specs,
        out_shape=out_shape,
        input_output_aliases={0: 0},
        scratch_shapes=scratch,
        compiler_params=_cparams(("arbitrary", "arbitrary")),
        name="ffn_final" if final else "ffn",
    )(*operands)
    return outs if next_w is not None else outs[0]


def _ffn_cast_tile(x, kern, small, scratch, mods, g2, fg2, wi, wo, layer, tm, final):
    n_rows = x.shape[0]
    vec = lambda i, f: (0, 0)
    tc = FFN_TF_CAST
    n_fc = D_FF // tc
    return pl.pallas_call(
        functools.partial(kern, tile0=0, cast=True, host=0),
        grid=(1, n_fc),
        in_specs=[pl.BlockSpec((tm, D_MODEL), lambda i, f: (0, 0), pipeline_mode=pl.Buffered(1))] + small + [
            pl.BlockSpec((None, D_MODEL, tc), lambda i, f: (layer, 0, f)),
            pl.BlockSpec((None, D_MODEL, tc), lambda i, f: (layer, 0, n_fc + f)),
            pl.BlockSpec((None, tc, D_MODEL), lambda i, f: (layer, f, 0)),
            pl.BlockSpec((1, D_MODEL), vec)],
        out_specs=[pl.BlockSpec((tm, D_MODEL), lambda i, f: (0, 0)),
                   pl.BlockSpec((D_MODEL, tc), lambda i, f: (0, f)),
                   pl.BlockSpec((D_MODEL, tc), lambda i, f: (0, f)),
                   pl.BlockSpec((tc, D_MODEL), lambda i, f: (f, 0))],
        out_shape=[jax.ShapeDtypeStruct((n_rows, D_MODEL), F32),
                   jax.ShapeDtypeStruct((D_MODEL, D_FF), BF16),
                   jax.ShapeDtypeStruct((D_MODEL, D_FF), BF16),
                   jax.ShapeDtypeStruct((D_FF, D_MODEL), BF16)],
        input_output_aliases={0: 0},
        scratch_shapes=scratch,
        compiler_params=_cparams(("arbitrary", "arbitrary")),
        name="ffn_cast_final" if final else "ffn_cast",
    )(x, mods, g2, wi, wi, wo, fg2)


def _rope(t, cos, sin):
    lane = lax.broadcasted_iota(jnp.int32, (1, HEAD_DIM), 1)
    first_half = (lane % (2 * ROPE_PAIRS)) < ROPE_PAIRS
    outs = []
    for hd in range(t.shape[1] // HEAD_DIM):
        th = t[:, hd * HEAD_DIM:(hd + 1) * HEAD_DIM]
        partner = jnp.where(first_half, pltpu.roll(th, HEAD_DIM - ROPE_PAIRS, 1),
                            pltpu.roll(th, ROPE_PAIRS, 1))
        outs.append(th * cos + partner * sin)
    return jnp.concatenate(outs, axis=1)


def _inproj_kernel(*refs, tm, host):
    x_ref, m_ref, g_ref, w_ref, cos_ref, sin_ref = refs[:6]
    o_ref, xn_ref = refs[6 + host], refs[-1]
    for src, dst in zip(refs[6:6 + host], refs[7 + host:-1]):
        dst[...] = src[...].astype(BF16)
    j = pl.program_id(1)
    q_blocks = K_OFF // PROJ_TN
    plain_end = GATE_OFF // PROJ_TN

    def project(epilogue):
        for c0 in range(0, PROJ_TN, PROJ_CHUNK):
            cs = slice(c0, c0 + PROJ_CHUNK)
            o_ref[:, cs] = epilogue(_dot(xn_ref[...], w_ref[:, cs]), c0).astype(BF16)

    def rope(acc):
        return _rope(acc, cos_ref[...], sin_ref[...])

    @pl.when(j == 0)
    def _():
        _norm_mod_rows(x_ref, xn_ref, g_ref, m_ref, 3, pl.program_id(0) * tm, tm)

    @pl.when(j < q_blocks)
    def _():
        project(lambda acc, c0: rope(acc) * (HEAD_DIM ** -0.5 * LOG2E))

    @pl.when(j == q_blocks)
    def _():
        project(lambda acc, c0: rope(acc) if c0 < V_OFF - K_OFF else acc)

    @pl.when((j > q_blocks) & (j < plain_end))
    def _():
        project(lambda acc, c0: acc)

    @pl.when(j >= plain_end)
    def _():
        project(lambda acc, c0: 0.5 * jnp.tanh(0.5 * acc) + 0.5)


def _inproj(x, mods, norm_g, w_bf, layer, cos, sin, ffn_wi, ffn_wo):
    tm = PROJ_TM
    n_i, n_j = ROWS // tm, IN_COLS // PROJ_TN
    row = lambda width: pl.BlockSpec((tm, width), lambda i, j: (i, 0))
    step = lambda i, j: i * n_j + j
    hosted = [_hosted_cast(ffn_wi, layer, 1, HOST_TN, step, n_i * n_j),
              _hosted_cast(ffn_wo, layer, 0, HOST_TN // 2, step, n_i * n_j)]
    outs = pl.pallas_call(
        functools.partial(_inproj_kernel, tm=tm, host=len(hosted)),
        grid=(n_i, n_j),
        in_specs=[row(D_MODEL)] + _layer_vector_specs(layer, 1) + [
            pl.BlockSpec((D_MODEL, PROJ_TN), lambda i, j: (0, j)),
            row(HEAD_DIM), row(HEAD_DIM)] + [h[0] for h in hosted],
        out_specs=[pl.BlockSpec((tm, PROJ_TN), lambda i, j: (i, j))] + [h[1] for h in hosted],
        out_shape=[jax.ShapeDtypeStruct((ROWS, IN_COLS), BF16)] + [h[2] for h in hosted],
        scratch_shapes=[pltpu.VMEM((tm, D_MODEL), BF16)],
        compiler_params=_cparams(("arbitrary", "arbitrary")),
        name="inproj",
    )(x, mods, norm_g.reshape(DEPTH * 3, 1, D_MODEL), w_bf, cos, sin, ffn_wi, ffn_wo)
    return outs[0], (outs[1], outs[2])


def _rope_tables():
    pos = np.arange(SEQ)
    inv = ROPE_BASE ** (-np.arange(ROPE_PAIRS, dtype=np.float64) / ROPE_PAIRS)
    ang_r = (pos // GRID_W)[:, None] * inv
    ang_c = (pos % GRID_W)[:, None] * inv
    cos = np.concatenate([np.cos(ang_r), np.cos(ang_r), np.cos(ang_c), np.cos(ang_c)], axis=1)
    sin = np.concatenate([-np.sin(ang_r), np.sin(ang_r), -np.sin(ang_c), np.sin(ang_c)], axis=1)
    cos = np.concatenate([cos, np.ones((CTX_LEN, HEAD_DIM))], axis=0)
    sin = np.concatenate([sin, np.zeros((CTX_LEN, HEAD_DIM))], axis=0)
    return jnp.asarray(cos, F32), jnp.asarray(sin, F32)


def _attn_kernel(sink_ref, q_ref, kp_ref, kc_ref, kn_ref, vp_ref, vc_ref, vn_ref, kx_ref, vx_ref,
                 *rest, layer):
    n_w = len(MIX_WEIGHT_ROWS)
    w_in, o_ref, w_out = rest[:n_w], rest[n_w], rest[n_w + 1:]
    for src, dst in zip(w_in, w_out):
        dst[...] = src[...].astype(BF16)
    n = pl.program_id(0)
    n_lat = SEQ // BLOCK
    r = lax.broadcasted_iota(jnp.int32, (BLOCK, BLOCK), 0)
    c = lax.broadcasted_iota(jnp.int32, (BLOCK, BLOCK), 1)
    is_lat = n < n_lat
    prev_ok = (c >= r) & (n >= 1) & is_lat
    next_ok = (c <= r) & (n < n_lat - 1)
    n_keys = 3 * BLOCK + CTX_LEN
    ones = jnp.ones((n_keys, HEAD_DIM), BF16)
    contract_last = (((1,), (1,)), ((), ()))
    for h in range(N_KV_HEADS):
        hs = slice(h * HEAD_DIM, (h + 1) * HEAD_DIM)
        qg = jnp.concatenate(
            [q_ref[:, (h * GQA_GROUP + g) * HEAD_DIM:(h * GQA_GROUP + g + 1) * HEAD_DIM]
             for g in range(GQA_GROUP)], axis=0)
        kb = jnp.concatenate([kp_ref[:, hs], kc_ref[:, hs], kn_ref[:, hs], kx_ref[:, hs]], axis=0)
        vb = jnp.concatenate([vp_ref[:, hs], vc_ref[:, hs], vn_ref[:, hs], vx_ref[:, hs]], axis=0)
        vb = jnp.concatenate([vb, ones], axis=1)
        s = lax.dot_general(qg, kb, contract_last, preferred_element_type=F32)
        probs, maxes, sinks = [], [], []
        for g in range(GQA_GROUP):
            sg = s[g * BLOCK:(g + 1) * BLOCK]
            parts = [jnp.where(prev_ok, sg[:, 0:BLOCK], NEG_INF),
                     jnp.where(is_lat, sg[:, BLOCK:2 * BLOCK], NEG_INF),
                     jnp.where(next_ok, sg[:, 2 * BLOCK:3 * BLOCK], NEG_INF),
                     sg[:, 3 * BLOCK:4 * BLOCK], sg[:, 4 * BLOCK:5 * BLOCK]]
            sink = sink_ref[layer * N_Q_HEADS + h * GQA_GROUP + g] * LOG2E
            lane_max = jnp.maximum(jnp.maximum(jnp.maximum(parts[0], parts[1]),
                                               jnp.maximum(parts[2], parts[3])), parts[4])
            m = jnp.maximum(jnp.max(lane_max, axis=-1, keepdims=True), sink)
            probs.append(jnp.concatenate([jnp.exp2(pt - m).astype(BF16) for pt in parts], axis=1))
            maxes.append(m)
            sinks.append(sink)
        oe = _dot(jnp.concatenate(probs, axis=0), vb)
        for g in range(GQA_GROUP):
            hq = h * GQA_GROUP + g
            og = oe[g * BLOCK:(g + 1) * BLOCK]
            denom = og[:, HEAD_DIM:HEAD_DIM + 1] + jnp.exp2(sinks[g] - maxes[g])
            o_ref[:, hq * HEAD_DIM:(hq + 1) * HEAD_DIM] = (og[:, :HEAD_DIM] / denom).astype(BF16)


def _attention(p, sinks, layer, n_rows, mix_weights):
    n_blocks = n_rows // BLOCK
    last = ROWS // BLOCK - 1
    kcol, vcol = K_OFF // 512, V_OFF // 512
    ctx_blk = SEQ // CTX_LEN
    kv = lambda col, off: pl.BlockSpec(
        (BLOCK, 512), lambda n, *_: (jnp.clip(n + off, 0, last), col))
    n_slabs = SEQ // BLOCK
    slab = lambda n: jnp.minimum(n, n_slabs - 1)
    w_in_specs, w_out_specs, w_out_shapes = [], [], []
    for w, rows in zip(mix_weights, MIX_WEIGHT_ROWS):
        assert w.shape[1] == rows
        sr, width = rows // n_slabs, w.shape[2]
        w_in_specs.append(pl.BlockSpec((None, sr, width), lambda n, *_: (layer, slab(n), 0)))
        w_out_specs.append(pl.BlockSpec((sr, width), lambda n, *_: (slab(n), 0)))
        w_out_shapes.append(jax.ShapeDtypeStruct((rows, width), BF16))
    outs = pl.pallas_call(
        functools.partial(_attn_kernel, layer=layer),
        grid_spec=pltpu.PrefetchScalarGridSpec(
            num_scalar_prefetch=1,
            grid=(n_blocks,),
            in_specs=[
                pl.BlockSpec((BLOCK, D_MODEL), lambda n, *_: (n, 0)),
                kv(kcol, -1), kv(kcol, 0), kv(kcol, 1),
                kv(vcol, -1), kv(vcol, 0), kv(vcol, 1),
                pl.BlockSpec((CTX_LEN, 512), lambda n, *_: (ctx_blk, kcol)),
                pl.BlockSpec((CTX_LEN, 512), lambda n, *_: (ctx_blk, vcol)),
            ] + w_in_specs,
            out_specs=[pl.BlockSpec((BLOCK, D_MODEL), lambda n, *_: (n, 0))] + w_out_specs,
        ),
        out_shape=[jax.ShapeDtypeStruct((n_rows, D_MODEL), BF16)] + w_out_shapes,
        compiler_params=_cparams(("arbitrary",)),
        name="attention",
    )(sinks, p, p, p, p, p, p, p, p, p, *mix_weights)
    return outs[0], outs[1:]


def _mix_kernel(x_ref, a_ref, u_ref, up_ref, un_ref, cb_ref, cc_ref, ccp_ref, ccn_ref,
                cx_ref, cxp_ref, cxn_ref, g0a_ref, g0b_ref, g1a_ref, g1b_ref, g2a_ref, g2b_ref,
                m_ref, pw_ref, ps_ref, cw_ref, wa_ref, wp_ref, wc_ref, wo_ref, o_ref, ub_ref):
    t = pl.program_id(0)
    tm = MIX_TM
    lat_tiles = SEQ // tm
    is_ctx = t >= lat_tiles
    seg_first = (t == 0) | is_ctx
    seg_last = (t == lat_tiles - 1) | is_ctx
    seg_len = jnp.where(is_ctx, CTX_LEN, SEQ)
    row = lax.broadcasted_iota(jnp.int32, (tm, 1), 0)
    pos = jnp.where(is_ctx, 0, t * tm) + row

    y_attn = _dot(a_ref[...], wa_ref[...])

    ub_ref[0:tm, :] = u_ref[...]
    ub_ref[tm:tm + HALO, :] = jnp.where(seg_last, jnp.zeros_like(un_ref[...]), un_ref[...])
    ub_ref[tm + HALO:POOL_K - HALO, :] = jnp.zeros((POOL_K - 2 * HALO - tm, POOL_WIDTH), BF16)
    ub_ref[POOL_K - HALO:POOL_K, :] = jnp.where(seg_first, jnp.zeros_like(up_ref[...]), up_ref[...])
    br = lax.broadcasted_iota(jnp.int32, (tm, POOL_K), 0)
    bc = lax.broadcasted_iota(jnp.int32, (tm, POOL_K), 1)
    d = bc - br
    d = jnp.where(d >= POOL_K // 2, d - POOL_K, d)
    pooled = []
    for gi, w in enumerate(POOL_WINDOWS):
        lo, hi = w // 2, w - w // 2 - 1
        gs = slice(gi * POOL_GROUP, (gi + 1) * POOL_GROUP)
        band = ((d >= -lo) & (d <= hi)).astype(BF16)
        wsum = _dot(band, ub_ref[:, gs])
        cnt = (jnp.minimum(pos + hi, seg_len - 1) - jnp.maximum(pos - lo, 0) + 1).astype(F32)
        y = wsum / cnt - u_ref[:, gs].astype(F32)
        pooled.append(_dot(y.astype(BF16), pw_ref[gs, :]))
    y_pool = jnp.concatenate(pooled, axis=1) * ps_ref[...]
    y_pool = _dot(y_pool.astype(BF16), wp_ref[...])

    z = cc_ref[...].astype(F32) * cx_ref[...].astype(F32)
    z_prev = ccp_ref[HALO - 1:HALO, :].astype(F32) * cxp_ref[HALO - 1:HALO, :].astype(F32)
    z_next = ccn_ref[0:1, :].astype(F32) * cxn_ref[0:1, :].astype(F32)
    z_prev = jnp.where(seg_first, 0.0, z_prev)
    z_next = jnp.where(seg_last, 0.0, z_next)
    zm1 = jnp.where(row == 0, z_prev, pltpu.roll(z, 1, 0))
    zp1 = jnp.where(row == tm - 1, z_next, pltpu.roll(z, tm - 1, 0))
    conv = zm1 * cw_ref[0:1, :] + z * cw_ref[1:2, :] + zp1 * cw_ref[2:3, :]
    y_conv = _dot((cb_ref[...].astype(F32) * conv).astype(BF16), wc_ref[...])

    halves = []
    for k, (g0, g1, g2) in enumerate(((g0a_ref, g1a_ref, g2a_ref), (g0b_ref, g1b_ref, g2b_ref))):
        cs = slice(k * GATE_TN, (k + 1) * GATE_TN)
        halves.append((g0[...].astype(F32) * y_attn[:, cs] + g1[...].astype(F32) * y_pool[:, cs]
                       + g2[...].astype(F32) * y_conv[:, cs]).astype(BF16))
    out = _dot(jnp.concatenate(halves, axis=1), wo_ref[...])
    o_ref[...] = x_ref[...] + _mod_rows(m_ref, 5, is_ctx) * out


def _mix(x, p, attn, mods, layer, pool_scale, conv_w, weights_bf, n_rows):
    w_attn_out, w_pool_out, w_conv_out, w_o, pool_w = weights_bf
    tm = MIX_TM
    n_tiles = n_rows // tm
    hb = tm // HALO
    last_halo = ROWS // HALO - 1
    tile = lambda width, col: pl.BlockSpec((tm, width), lambda t: (t, col))
    prev = lambda col: pl.BlockSpec((HALO, POOL_WIDTH), lambda t: (jnp.maximum(t * hb - 1, 0), col))
    nxt = lambda col: pl.BlockSpec((HALO, POOL_WIDTH), lambda t: (jnp.minimum((t + 1) * hb, last_halo), col))
    const = lambda shape: pl.BlockSpec(shape, lambda t: (0,) * len(shape), pipeline_mode=pl.Buffered(1))
    ucol, cbcol, cccol, cxcol = POOL_OFF // 1024, CB_OFF // 1024, CC_OFF // 1024, CX_OFF // 1024
    gcol = GATE_OFF // GATE_TN
    return pl.pallas_call(
        _mix_kernel,
        grid=(n_tiles,),
        in_specs=[
            tile(D_MODEL, 0),
            tile(D_MODEL, 0),
            tile(1024, ucol), prev(ucol), nxt(ucol),
            tile(1024, cbcol),
            tile(1024, cccol), prev(cccol), nxt(cccol),
            tile(1024, cxcol), prev(cxcol), nxt(cxcol),
            *[tile(GATE_TN, gcol + k) for k in range(3 * D_MODEL // GATE_TN)],
            pl.BlockSpec((None, 8, N_MODS * D_MODEL), lambda t: (layer, 0, 0), pipeline_mode=pl.Buffered(1)),
            const((POOL_WIDTH, POOL_GROUP)),
            pl.BlockSpec((None, 1, POOL_WIDTH), lambda t: (layer, 0, 0), pipeline_mode=pl.Buffered(1)),
            pl.BlockSpec((None, 3, CONV_WIDTH), lambda t: (layer, 0, 0), pipeline_mode=pl.Buffered(1)),
            const((D_MODEL, D_MODEL)),
            const((POOL_WIDTH, D_MODEL)),
            const((CONV_WIDTH, D_MODEL)),
            const((D_MODEL, D_MODEL)),
        ],
        out_specs=pl.BlockSpec((tm, D_MODEL), lambda t: (t, 0)),
        out_shape=jax.ShapeDtypeStruct((n_rows, D_MODEL), F32),
        scratch_shapes=[pltpu.VMEM((POOL_K, POOL_WIDTH), BF16)],
        compiler_params=_cparams(("arbitrary",)),
        name="mix",
    )(x, attn, *([p] * 16), mods, pool_w, pool_scale.reshape(DEPTH, 1, POOL_WIDTH),
      conv_w, w_attn_out, w_pool_out, w_conv_out, w_o)


def kernel(x, c, ctx, c_ctx, w_ada, b_ada, norm_g, ffn1_wi, ffn1_wo, w_in, attn_sink, pool_w, pool_scale,
           conv_w, w_attn_out, w_pool_out, w_conv_out, w_o, ffn2_wi, ffn2_wo, final_g):
    assert x.shape == (1, SEQ, D_MODEL) and ctx.shape == (1, CTX_LEN, D_MODEL)
    cond = jnp.concatenate([c.reshape(1, D_MODEL), c_ctx.reshape(1, D_MODEL),
                            jnp.zeros((6, D_MODEL), F32)], axis=0)
    mods = _ada(cond, w_ada, b_ada)
    cos, sin = _rope_tables()
    mix_weights = (w_attn_out, w_pool_out, w_conv_out, w_o,
                   pool_w.reshape(DEPTH, POOL_WIDTH, POOL_GROUP))
    sinks = attn_sink.reshape(DEPTH * N_Q_HEADS)

    h = jnp.concatenate([x[0], ctx[0]], axis=0)
    for l in range(DEPTH):
        last = l == DEPTH - 1
        h, w_in_bf = _ffn(h, FFN_TM, FFN_TF, mods, norm_g, ffn1_wi, ffn1_wo, l, 0, final_g, next_w=w_in)
        p, ffn2_bf = _inproj(h, mods, norm_g, w_in_bf, l, cos, sin, ffn2_wi, ffn2_wo)
        n_rows = SEQ if last else ROWS
        attn, mix_bf = _attention(p, sinks, l, n_rows, mix_weights)
        h = _mix(h, p, attn, mods, l, pool_scale, conv_w, mix_bf, n_rows)
        h = _ffn(h, FFN_TM_LATENT if last else FFN_TM, FFN_TF, mods, norm_g, ffn2_wi, ffn2_wo, l, 6, final_g,
                 final=last, weights_bf=ffn2_bf)
    return h.reshape(1, SEQ, D_MODEL)
```

```python
import functools

import jax
import jax.numpy as jnp
import numpy as np
from jax import lax
from jax.experimental import pallas as pl
from jax.experimental.pallas import tpu as pltpu

D_MODEL = 2048
SEQ = 8192
CTX_LEN = 256
ROWS = SEQ + CTX_LEN
DEPTH = 2
GRID_W = 64
HEAD_DIM = 128
N_Q_HEADS = 16
N_KV_HEADS = 4
GQA_GROUP = 4
WINDOW = 128
BLOCK = 128
ROPE_PAIRS = 32
ROPE_BASE = 10000.0
POOL_WINDOWS = (2, 4, 8, 16)
POOL_WIDTH = 1024
POOL_GROUP = 256
CONV_WIDTH = 1024
D_FF = 5632
N_MODS = 9
EPS = 1e-6
NEG_INF = -1e30
LOG2E = 1.4426950408889634

K_OFF = 2048
V_OFF = 2560
POOL_OFF = 3072
CB_OFF = 4096
CC_OFF = 5120
CX_OFF = 6144
GATE_OFF = 7168
IN_COLS = GATE_OFF + 3 * D_MODEL

BF16 = jnp.bfloat16
F32 = jnp.float32

VMEM_LIMIT_BYTES = 62 * 1024 * 1024

FFN_TM = 1056
FFN_TM_LATENT = 1024
PROJ_TM = 1408
FFN_TF = 512
HOST_TN = 256
FFN_TF_CAST = 256
PROJ_TN = 1024
PROJ_CHUNK = 512
ROW_CHUNK = 32
ROWS_UNROLL = 4
HIDDEN_CHUNK = 256
COL_CHUNK = 512
MIX_TM = 256
MIX_WEIGHT_ROWS = (2048, 1024, 1024, 2048, 1024)
GATE_TN = 1024
HALO = 16
POOL_K = 384


def _cparams(sem):
    return pltpu.CompilerParams(dimension_semantics=sem, vmem_limit_bytes=VMEM_LIMIT_BYTES)


def _layer_vector_specs(layer, norm_idx):
    once = pl.Buffered(1)
    return [pl.BlockSpec((None, 8, N_MODS * D_MODEL), lambda *_: (layer, 0, 0), pipeline_mode=once),
            pl.BlockSpec((None, 1, D_MODEL), lambda *_: (layer * 3 + norm_idx, 0, 0), pipeline_mode=once)]


def _hosted_cast(w, layer, axis, size, step, n_steps):
    rows, cols = w.shape[1:]
    n_slabs = (rows, cols)[axis] // size
    assert n_slabs * size == (rows, cols)[axis] and n_slabs <= n_steps
    block = (size, cols) if axis == 0 else (rows, size)

    def index(*ids):
        k = jnp.minimum(step(*ids), n_slabs - 1)
        return (k, 0) if axis == 0 else (0, k)

    return (pl.BlockSpec((None,) + block, lambda *ids: (layer,) + index(*ids)),
            pl.BlockSpec(block, index),
            jax.ShapeDtypeStruct((rows, cols), BF16))


def _dot(a, b):
    return jnp.dot(a, b, preferred_element_type=F32)


def _norm_mod(x, g, shift, scale):
    y = x * lax.rsqrt(jnp.mean(x * x, axis=-1, keepdims=True) + EPS)
    return y * (g * (1.0 + scale)) + shift


def _mod_rows(m_ref, k, is_ctx):
    lo, hi = k * D_MODEL, (k + 1) * D_MODEL
    return jnp.where(is_ctx, m_ref[1:2, lo:hi], m_ref[0:1, lo:hi])


assert SEQ % ROW_CHUNK == 0


def _is_ctx_rows(first_row):
    return first_row >= SEQ


def _rows_loop(tm, body, unroll):
    def step(r, carry):
        body(pl.multiple_of(r * ROW_CHUNK, ROW_CHUNK))
        return carry
    lax.fori_loop(0, tm // ROW_CHUNK, step, 0, unroll=unroll)


def _norm_mod_rows(x_ref, xm_ref, g_ref, m_ref, k_shift, row0, tm):
    def rows(r0):
        rs = pl.ds(r0, ROW_CHUNK)
        is_ctx = _is_ctx_rows(row0 + r0)
        xm_ref[rs, :] = _norm_mod(x_ref[rs, :], g_ref[...], _mod_rows(m_ref, k_shift, is_ctx),
                                  _mod_rows(m_ref, k_shift + 1, is_ctx)).astype(BF16)
    _rows_loop(tm, rows, ROWS_UNROLL)


def _ada_kernel(c_ref, w_ref, b_ref, o_ref):
    c = c_ref[...]
    s = c * jax.nn.sigmoid(c)
    o_ref[...] = _dot(s.astype(BF16), w_ref[...].astype(BF16)) + b_ref[...]


def _ada(cond, w_ada, b_ada):
    tn = 1024
    n_cols = N_MODS * D_MODEL
    return pl.pallas_call(
        _ada_kernel,
        grid=(DEPTH, n_cols // tn),
        in_specs=[
            pl.BlockSpec((8, D_MODEL), lambda l, j: (0, 0)),
            pl.BlockSpec((None, D_MODEL, tn), lambda l, j: (l, 0, j)),
            pl.BlockSpec((None, 1, tn), lambda l, j: (l, 0, j)),
        ],
        out_specs=pl.BlockSpec((None, 8, tn), lambda l, j: (l, 0, j)),
        out_shape=jax.ShapeDtypeStruct((DEPTH, 8, n_cols), F32),
        compiler_params=_cparams(("arbitrary", "arbitrary")),
        name="ada",
    )(cond, w_ada, b_ada.reshape(DEPTH, 1, n_cols))


def _ffn_kernel(*refs, tm, tile0, k0, final, cast, host):
    if cast:
        (x_ref, m_ref, g_ref, wig_in, wiu_in, wo_in, fg_ref,
         o_ref, wig_ref, wiu_ref, wo_ref, xm_ref) = refs
    else:
        x_ref, m_ref, g_ref, wig_ref, wiu_ref, wo_ref, fg_ref = refs[:7]
        o_ref, xm_ref = refs[7 + host], refs[-1]
        for src, dst in zip(refs[7:7 + host], refs[8 + host:-1]):
            dst[...] = src[...].astype(BF16)
    row0 = (pl.program_id(0) + tile0) * tm
    f = pl.program_id(1)

    @pl.when(f == 0)
    def _():
        _norm_mod_rows(x_ref, xm_ref, g_ref, m_ref, k0, row0, tm)
        o_ref[...] = jnp.zeros_like(o_ref)

    if cast:
        wig_ref[...] = wig_in[...].astype(BF16)
        wiu_ref[...] = wiu_in[...].astype(BF16)
        wo_ref[...] = wo_in[...].astype(BF16)
    xm = xm_ref[...]
    parts = []
    for h0 in range(0, wig_ref.shape[1], HIDDEN_CHUNK):
        hs = slice(h0, h0 + HIDDEN_CHUNK)
        g = _dot(xm, wig_ref[:, hs])
        u = _dot(xm, wiu_ref[:, hs])
        parts.append(((g * jax.nn.sigmoid(g)) * u).astype(BF16))
    a = parts[0] if len(parts) == 1 else jnp.concatenate(parts, axis=1)
    for c0 in range(0, D_MODEL, COL_CHUNK):
        o_ref[:, c0:c0 + COL_CHUNK] += _dot(a, wo_ref[:, c0:c0 + COL_CHUNK])

    @pl.when(f == pl.num_programs(1) - 1)
    def _():
        def rows(r0):
            rs = pl.ds(r0, ROW_CHUNK)
            gate = _mod_rows(m_ref, k0 + 2, _is_ctx_rows(row0 + r0))
            y = x_ref[rs, :] + (0.5 * gate) * o_ref[rs, :]
            if final:
                y = (y * lax.rsqrt(jnp.mean(y * y, axis=-1, keepdims=True) + EPS)) * fg_ref[...]
            o_ref[rs, :] = y
        _rows_loop(tm, rows, ROWS_UNROLL if final else 1)


def _ffn(x, tm, tf, mods, norm_g, wi, wo, layer, k0, final_g, final=False, hosted=(), weights_bf=None):
    n_rows = x.shape[0]
    kern = functools.partial(_ffn_kernel, tm=tm, k0=k0, final=final)
    vec = lambda i, f: (0, 0)
    small = _layer_vector_specs(layer, k0 // 3)
    scratch = [pltpu.VMEM((tm, D_MODEL), BF16)]
    g2, fg2 = norm_g.reshape(DEPTH * 3, 1, D_MODEL), final_g.reshape(1, D_MODEL)
    n_f = D_FF // tf
    if weights_bf is not None:
        wig = wiu = weights_bf[0]
        wo_bf, up_block0, tile0 = weights_bf[1], n_f, 0
    else:
        x, wig, wiu, wo_bf = _ffn_cast_tile(x, kern, small, scratch, mods, g2, fg2, wi, wo, layer, tm, final)
        up_block0, tile0 = 0, 1

    n_i = n_rows // tm - tile0
    in_specs = [pl.BlockSpec((tm, D_MODEL), lambda i, f: (i + tile0, 0))] + small + [
        pl.BlockSpec((D_MODEL, tf), lambda i, f: (0, f)),
        pl.BlockSpec((D_MODEL, tf), lambda i, f: (0, up_block0 + f)),
        pl.BlockSpec((tf, D_MODEL), lambda i, f: (f, 0)),
        pl.BlockSpec((1, D_MODEL), vec)]
    out_specs = [pl.BlockSpec((tm, D_MODEL), lambda i, f: (i + tile0, 0))]
    out_shape = [jax.ShapeDtypeStruct((n_rows, D_MODEL), F32)]
    operands = [x, mods, g2, wig, wiu, wo_bf, fg2]
    for w, w_layer, axis, size in hosted:
        spec_in, spec_out, shape = _hosted_cast(w, w_layer, axis, size, lambda i, f: i * n_f + f, n_i * n_f)
        in_specs.append(spec_in)
        out_specs.append(spec_out)
        out_shape.append(shape)
        operands.append(w)
    outs = pl.pallas_call(
        functools.partial(kern, tile0=tile0, cast=False, host=len(hosted)),
        grid=(n_i, n_f),
        in_specs=in_specs,
        out_specs=out_specs,
        out_shape=out_shape,
        input_output_aliases={0: 0},
        scratch_shapes=scratch,
        compiler_params=_cparams(("arbitrary", "arbitrary")),
        name="ffn_final" if final else "ffn",
    )(*operands)
    return outs[0], tuple(outs[1:])


def _ffn_cast_tile(x, kern, small, scratch, mods, g2, fg2, wi, wo, layer, tm, final):
    n_rows = x.shape[0]
    vec = lambda i, f: (0, 0)
    tc = FFN_TF_CAST
    n_fc = D_FF // tc
    return pl.pallas_call(
        functools.partial(kern, tile0=0, cast=True, host=0),
        grid=(1, n_fc),
        in_specs=[pl.BlockSpec((tm, D_MODEL), lambda i, f: (0, 0), pipeline_mode=pl.Buffered(1))] + small + [
            pl.BlockSpec((None, D_MODEL, tc), lambda i, f: (layer, 0, f)),
            pl.BlockSpec((None, D_MODEL, tc), lambda i, f: (layer, 0, n_fc + f)),
            pl.BlockSpec((None, tc, D_MODEL), lambda i, f: (layer, f, 0)),
            pl.BlockSpec((1, D_MODEL), vec)],
        out_specs=[pl.BlockSpec((tm, D_MODEL), lambda i, f: (0, 0)),
                   pl.BlockSpec((D_MODEL, tc), lambda i, f: (0, f)),
                   pl.BlockSpec((D_MODEL, tc), lambda i, f: (0, f)),
                   pl.BlockSpec((tc, D_MODEL), lambda i, f: (f, 0))],
        out_shape=[jax.ShapeDtypeStruct((n_rows, D_MODEL), F32),
                   jax.ShapeDtypeStruct((D_MODEL, D_FF), BF16),
                   jax.ShapeDtypeStruct((D_MODEL, D_FF), BF16),
                   jax.ShapeDtypeStruct((D_FF, D_MODEL), BF16)],
        input_output_aliases={0: 0},
        scratch_shapes=scratch,
        compiler_params=_cparams(("arbitrary", "arbitrary")),
        name="ffn_cast_final" if final else "ffn_cast",
    )(x, mods, g2, wi, wi, wo, fg2)


def _rope(t, cos, sin):
    lane = lax.broadcasted_iota(jnp.int32, (1, HEAD_DIM), 1)
    first_half = (lane % (2 * ROPE_PAIRS)) < ROPE_PAIRS
    outs = []
    for hd in range(t.shape[1] // HEAD_DIM):
        th = t[:, hd * HEAD_DIM:(hd + 1) * HEAD_DIM]
        partner = jnp.where(first_half, pltpu.roll(th, HEAD_DIM - ROPE_PAIRS, 1),
                            pltpu.roll(th, ROPE_PAIRS, 1))
        outs.append(th * cos + partner * sin)
    return jnp.concatenate(outs, axis=1)


def _inproj_kernel(*refs, tm, host):
    x_ref, m_ref, g_ref, w_ref, cos_ref, sin_ref = refs[:6]
    o_ref, xn_ref = refs[6 + host], refs[-1]
    j = pl.program_id(1)
    q_blocks = K_OFF // PROJ_TN
    plain_end = GATE_OFF // PROJ_TN

    def project(epilogue):
        for src, dst in zip(refs[6:6 + host], refs[7 + host:-1]):
            dst[...] = src[...].astype(BF16)
        for c0 in range(0, PROJ_TN, PROJ_CHUNK):
            cs = slice(c0, c0 + PROJ_CHUNK)
            o_ref[:, cs] = epilogue(_dot(xn_ref[...], w_ref[:, cs]), c0).astype(BF16)

    def rope(acc):
        return _rope(acc, cos_ref[...], sin_ref[...])

    @pl.when(j == 0)
    def _():
        _norm_mod_rows(x_ref, xn_ref, g_ref, m_ref, 3, pl.program_id(0) * tm, tm)

    @pl.when(j < q_blocks)
    def _():
        project(lambda acc, c0: rope(acc) * (HEAD_DIM ** -0.5 * LOG2E))

    @pl.when(j == q_blocks)
    def _():
        project(lambda acc, c0: rope(acc) if c0 < V_OFF - K_OFF else acc)

    @pl.when((j > q_blocks) & (j < plain_end))
    def _():
        project(lambda acc, c0: acc)

    @pl.when(j >= plain_end)
    def _():
        project(lambda acc, c0: 0.5 * jnp.tanh(0.5 * acc) + 0.5)


def _inproj(x, mods, norm_g, w_bf, layer, cos, sin, ffn_wi, ffn_wo):
    tm = PROJ_TM
    n_i, n_j = ROWS // tm, IN_COLS // PROJ_TN
    row = lambda width: pl.BlockSpec((tm, width), lambda i, j: (i, 0))
    step = lambda i, j: i * n_j + j
    hosted = [_hosted_cast(ffn_wi, layer, 1, HOST_TN, step, n_i * n_j),
              _hosted_cast(ffn_wo, layer, 0, HOST_TN // 2, step, n_i * n_j)]
    outs = pl.pallas_call(
        functools.partial(_inproj_kernel, tm=tm, host=len(hosted)),
        grid=(n_i, n_j),
        in_specs=[row(D_MODEL)] + _layer_vector_specs(layer, 1) + [
            pl.BlockSpec((D_MODEL, PROJ_TN), lambda i, j: (0, j)),
            row(HEAD_DIM), row(HEAD_DIM)] + [h[0] for h in hosted],
        out_specs=[pl.BlockSpec((tm, PROJ_TN), lambda i, j: (i, j))] + [h[1] for h in hosted],
        out_shape=[jax.ShapeDtypeStruct((ROWS, IN_COLS), BF16)] + [h[2] for h in hosted],
        scratch_shapes=[pltpu.VMEM((tm, D_MODEL), BF16)],
        compiler_params=_cparams(("arbitrary", "arbitrary")),
        name="inproj",
    )(x, mods, norm_g.reshape(DEPTH * 3, 1, D_MODEL), w_bf, cos, sin, ffn_wi, ffn_wo)
    return outs[0], (outs[1], outs[2])


def _rope_tables():
    pos = np.arange(SEQ)
    inv = ROPE_BASE ** (-np.arange(ROPE_PAIRS, dtype=np.float64) / ROPE_PAIRS)
    ang_r = (pos // GRID_W)[:, None] * inv
    ang_c = (pos % GRID_W)[:, None] * inv
    cos = np.concatenate([np.cos(ang_r), np.cos(ang_r), np.cos(ang_c), np.cos(ang_c)], axis=1)
    sin = np.concatenate([-np.sin(ang_r), np.sin(ang_r), -np.sin(ang_c), np.sin(ang_c)], axis=1)
    cos = np.concatenate([cos, np.ones((CTX_LEN, HEAD_DIM))], axis=0)
    sin = np.concatenate([sin, np.zeros((CTX_LEN, HEAD_DIM))], axis=0)
    return jnp.asarray(cos, F32), jnp.asarray(sin, F32)


def _attn_kernel(sink_ref, q_ref, kp_ref, kc_ref, kn_ref, vp_ref, vc_ref, vn_ref, kx_ref, vx_ref,
                 *rest, layer):
    n_w = len(MIX_WEIGHT_ROWS)
    w_in, o_ref, w_out = rest[:n_w], rest[n_w], rest[n_w + 1:]
    for src, dst in zip(w_in, w_out):
        dst[...] = src[...].astype(BF16)
    n = pl.program_id(0)
    n_lat = SEQ // BLOCK
    r = lax.broadcasted_iota(jnp.int32, (BLOCK, BLOCK), 0)
    c = lax.broadcasted_iota(jnp.int32, (BLOCK, BLOCK), 1)
    is_lat = n < n_lat
    prev_ok = (c >= r) & (n >= 1) & is_lat
    next_ok = (c <= r) & (n < n_lat - 1)
    n_keys = 3 * BLOCK + CTX_LEN
    ones = jnp.ones((n_keys, HEAD_DIM), BF16)
    contract_last = (((1,), (1,)), ((), ()))
    for h in range(N_KV_HEADS):
        hs = slice(h * HEAD_DIM, (h + 1) * HEAD_DIM)
        qg = jnp.concatenate(
            [q_ref[:, (h * GQA_GROUP + g) * HEAD_DIM:(h * GQA_GROUP + g + 1) * HEAD_DIM]
             for g in range(GQA_GROUP)], axis=0)
        kb = jnp.concatenate([kp_ref[:, hs], kc_ref[:, hs], kn_ref[:, hs], kx_ref[:, hs]], axis=0)
        vb = jnp.concatenate([vp_ref[:, hs], vc_ref[:, hs], vn_ref[:, hs], vx_ref[:, hs]], axis=0)
        vb = jnp.concatenate([vb, ones], axis=1)
        s = lax.dot_general(qg, kb, contract_last, preferred_element_type=F32)
        probs, maxes, sinks = [], [], []
        for g in range(GQA_GROUP):
            sg = s[g * BLOCK:(g + 1) * BLOCK]
            parts = [jnp.where(prev_ok, sg[:, 0:BLOCK], NEG_INF),
                     jnp.where(is_lat, sg[:, BLOCK:2 * BLOCK], NEG_INF),
                     jnp.where(next_ok, sg[:, 2 * BLOCK:3 * BLOCK], NEG_INF),
                     sg[:, 3 * BLOCK:4 * BLOCK], sg[:, 4 * BLOCK:5 * BLOCK]]
            sink = sink_ref[layer * N_Q_HEADS + h * GQA_GROUP + g] * LOG2E
            lane_max = jnp.maximum(jnp.maximum(jnp.maximum(parts[0], parts[1]),
                                               jnp.maximum(parts[2], parts[3])), parts[4])
            m = jnp.maximum(jnp.max(lane_max, axis=-1, keepdims=True), sink)
            probs.append(jnp.concatenate([jnp.exp2(pt - m).astype(BF16) for pt in parts], axis=1))
            maxes.append(m)
            sinks.append(sink)
        oe = _dot(jnp.concatenate(probs, axis=0), vb)
        for g in range(GQA_GROUP):
            hq = h * GQA_GROUP + g
            og = oe[g * BLOCK:(g + 1) * BLOCK]
            denom = og[:, HEAD_DIM:HEAD_DIM + 1] + jnp.exp2(sinks[g] - maxes[g])
            o_ref[:, hq * HEAD_DIM:(hq + 1) * HEAD_DIM] = (og[:, :HEAD_DIM] / denom).astype(BF16)


def _attention(p, sinks, layer, n_rows, mix_weights):
    n_blocks = n_rows // BLOCK
    last = ROWS // BLOCK - 1
    kcol, vcol = K_OFF // 512, V_OFF // 512
    ctx_blk = SEQ // CTX_LEN
    kv = lambda col, off: pl.BlockSpec(
        (BLOCK, 512), lambda n, *_: (jnp.clip(n + off, 0, last), col))
    n_slabs = SEQ // BLOCK
    slab = lambda n: jnp.minimum(n, n_slabs - 1)
    w_in_specs, w_out_specs, w_out_shapes = [], [], []
    for w, rows in zip(mix_weights, MIX_WEIGHT_ROWS):
        assert w.shape[1] == rows
        sr, width = rows // n_slabs, w.shape[2]
        w_in_specs.append(pl.BlockSpec((None, sr, width), lambda n, *_: (layer, slab(n), 0)))
        w_out_specs.append(pl.BlockSpec((sr, width), lambda n, *_: (slab(n), 0)))
        w_out_shapes.append(jax.ShapeDtypeStruct((rows, width), BF16))
    outs = pl.pallas_call(
        functools.partial(_attn_kernel, layer=layer),
        grid_spec=pltpu.PrefetchScalarGridSpec(
            num_scalar_prefetch=1,
            grid=(n_blocks,),
            in_specs=[
                pl.BlockSpec((BLOCK, D_MODEL), lambda n, *_: (n, 0)),
                kv(kcol, -1), kv(kcol, 0), kv(kcol, 1),
                kv(vcol, -1), kv(vcol, 0), kv(vcol, 1),
                pl.BlockSpec((CTX_LEN, 512), lambda n, *_: (ctx_blk, kcol)),
                pl.BlockSpec((CTX_LEN, 512), lambda n, *_: (ctx_blk, vcol)),
            ] + w_in_specs,
            out_specs=[pl.BlockSpec((BLOCK, D_MODEL), lambda n, *_: (n, 0))] + w_out_specs,
        ),
        out_shape=[jax.ShapeDtypeStruct((n_rows, D_MODEL), BF16)] + w_out_shapes,
        compiler_params=_cparams(("arbitrary",)),
        name="attention",
    )(sinks, p, p, p, p, p, p, p, p, p, *mix_weights)
    return outs[0], outs[1:]


def _mix_kernel(x_ref, a_ref, u_ref, up_ref, un_ref, cb_ref, cc_ref, ccp_ref, ccn_ref,
                cx_ref, cxp_ref, cxn_ref, g0a_ref, g0b_ref, g1a_ref, g1b_ref, g2a_ref, g2b_ref,
                m_ref, pw_ref, ps_ref, cw_ref, wa_ref, wp_ref, wc_ref, wo_ref, o_ref, ub_ref):
    t = pl.program_id(0)
    tm = MIX_TM
    lat_tiles = SEQ // tm
    is_ctx = t >= lat_tiles
    seg_first = (t == 0) | is_ctx
    seg_last = (t == lat_tiles - 1) | is_ctx
    seg_len = jnp.where(is_ctx, CTX_LEN, SEQ)
    row = lax.broadcasted_iota(jnp.int32, (tm, 1), 0)
    pos = jnp.where(is_ctx, 0, t * tm) + row

    y_attn = _dot(a_ref[...], wa_ref[...])

    ub_ref[0:tm, :] = u_ref[...]
    ub_ref[tm:tm + HALO, :] = jnp.where(seg_last, jnp.zeros_like(un_ref[...]), un_ref[...])
    ub_ref[tm + HALO:POOL_K - HALO, :] = jnp.zeros((POOL_K - 2 * HALO - tm, POOL_WIDTH), BF16)
    ub_ref[POOL_K - HALO:POOL_K, :] = jnp.where(seg_first, jnp.zeros_like(up_ref[...]), up_ref[...])
    br = lax.broadcasted_iota(jnp.int32, (tm, POOL_K), 0)
    bc = lax.broadcasted_iota(jnp.int32, (tm, POOL_K), 1)
    d = bc - br
    d = jnp.where(d >= POOL_K // 2, d - POOL_K, d)
    pooled = []
    for gi, w in enumerate(POOL_WINDOWS):
        lo, hi = w // 2, w - w // 2 - 1
        gs = slice(gi * POOL_GROUP, (gi + 1) * POOL_GROUP)
        band = ((d >= -lo) & (d <= hi)).astype(BF16)
        wsum = _dot(band, ub_ref[:, gs])
        cnt = (jnp.minimum(pos + hi, seg_len - 1) - jnp.maximum(pos - lo, 0) + 1).astype(F32)
        y = wsum / cnt - u_ref[:, gs].astype(F32)
        pooled.append(_dot(y.astype(BF16), pw_ref[gs, :]))
    y_pool = jnp.concatenate(pooled, axis=1) * ps_ref[...]
    y_pool = _dot(y_pool.astype(BF16), wp_ref[...])

    z = cc_ref[...].astype(F32) * cx_ref[...].astype(F32)
    z_prev = ccp_ref[HALO - 1:HALO, :].astype(F32) * cxp_ref[HALO - 1:HALO, :].astype(F32)
    z_next = ccn_ref[0:1, :].astype(F32) * cxn_ref[0:1, :].astype(F32)
    z_prev = jnp.where(seg_first, 0.0, z_prev)
    z_next = jnp.where(seg_last, 0.0, z_next)
    zm1 = jnp.where(row == 0, z_prev, pltpu.roll(z, 1, 0))
    zp1 = jnp.where(row == tm - 1, z_next, pltpu.roll(z, tm - 1, 0))
    conv = zm1 * cw_ref[0:1, :] + z * cw_ref[1:2, :] + zp1 * cw_ref[2:3, :]
    y_conv = _dot((cb_ref[...].astype(F32) * conv).astype(BF16), wc_ref[...])

    halves = []
    for k, (g0, g1, g2) in enumerate(((g0a_ref, g1a_ref, g2a_ref), (g0b_ref, g1b_ref, g2b_ref))):
        cs = slice(k * GATE_TN, (k + 1) * GATE_TN)
        halves.append((g0[...].astype(F32) * y_attn[:, cs] + g1[...].astype(F32) * y_pool[:, cs]
                       + g2[...].astype(F32) * y_conv[:, cs]).astype(BF16))
    out = _dot(jnp.concatenate(halves, axis=1), wo_ref[...])
    o_ref[...] = x_ref[...] + _mod_rows(m_ref, 5, is_ctx) * out


def _mix(x, p, attn, mods, layer, pool_scale, conv_w, weights_bf, n_rows):
    w_attn_out, w_pool_out, w_conv_out, w_o, pool_w = weights_bf
    tm = MIX_TM
    n_tiles = n_rows // tm
    hb = tm // HALO
    last_halo = ROWS // HALO - 1
    tile = lambda width, col: pl.BlockSpec((tm, width), lambda t: (t, col))
    prev = lambda col: pl.BlockSpec((HALO, POOL_WIDTH), lambda t: (jnp.maximum(t * hb - 1, 0), col))
    nxt = lambda col: pl.BlockSpec((HALO, POOL_WIDTH), lambda t: (jnp.minimum((t + 1) * hb, last_halo), col))
    const = lambda shape: pl.BlockSpec(shape, lambda t: (0,) * len(shape), pipeline_mode=pl.Buffered(1))
    ucol, cbcol, cccol, cxcol = POOL_OFF // 1024, CB_OFF // 1024, CC_OFF // 1024, CX_OFF // 1024
    gcol = GATE_OFF // GATE_TN
    return pl.pallas_call(
        _mix_kernel,
        grid=(n_tiles,),
        in_specs=[
            tile(D_MODEL, 0),
            tile(D_MODEL, 0),
            tile(1024, ucol), prev(ucol), nxt(ucol),
            tile(1024, cbcol),
            tile(1024, cccol), prev(cccol), nxt(cccol),
            tile(1024, cxcol), prev(cxcol), nxt(cxcol),
            *[tile(GATE_TN, gcol + k) for k in range(3 * D_MODEL // GATE_TN)],
            pl.BlockSpec((None, 8, N_MODS * D_MODEL), lambda t: (layer, 0, 0), pipeline_mode=pl.Buffered(1)),
            const((POOL_WIDTH, POOL_GROUP)),
            pl.BlockSpec((None, 1, POOL_WIDTH), lambda t: (layer, 0, 0), pipeline_mode=pl.Buffered(1)),
            pl.BlockSpec((None, 3, CONV_WIDTH), lambda t: (layer, 0, 0), pipeline_mode=pl.Buffered(1)),
            const((D_MODEL, D_MODEL)),
            const((POOL_WIDTH, D_MODEL)),
            const((CONV_WIDTH, D_MODEL)),
            const((D_MODEL, D_MODEL)),
        ],
        out_specs=pl.BlockSpec((tm, D_MODEL), lambda t: (t, 0)),
        out_shape=jax.ShapeDtypeStruct((n_rows, D_MODEL), F32),
        scratch_shapes=[pltpu.VMEM((POOL_K, POOL_WIDTH), BF16)],
        compiler_params=_cparams(("arbitrary",)),
        name="mix",
    )(x, attn, *([p] * 16), mods, pool_w, pool_scale.reshape(DEPTH, 1, POOL_WIDTH),
      conv_w, w_attn_out, w_pool_out, w_conv_out, w_o)


def kernel(x, c, ctx, c_ctx, w_ada, b_ada, norm_g, ffn1_wi, ffn1_wo, w_in, attn_sink, pool_w, pool_scale,
           conv_w, w_attn_out, w_pool_out, w_conv_out, w_o, ffn2_wi, ffn2_wo, final_g):
    assert x.shape == (1, SEQ, D_MODEL) and ctx.shape == (1, CTX_LEN, D_MODEL)
    cond = jnp.concatenate([c.reshape(1, D_MODEL), c_ctx.reshape(1, D_MODEL),
                            jnp.zeros((6, D_MODEL), F32)], axis=0)
    mods = _ada(cond, w_ada, b_ada)
    cos, sin = _rope_tables()
    mix_weights = (w_attn_out, w_pool_out, w_conv_out, w_o,
                   pool_w.reshape(DEPTH, POOL_WIDTH, POOL_GROUP))
    sinks = attn_sink.reshape(DEPTH * N_Q_HEADS)

    h = jnp.concatenate([x[0], ctx[0]], axis=0)
    ffn1_bf = None
    for l in range(DEPTH):
        last = l == DEPTH - 1
        h, (w_in_bf,) = _ffn(h, FFN_TM, FFN_TF, mods, norm_g, ffn1_wi, ffn1_wo, l, 0, final_g,
                             hosted=[(w_in, l, 1, HOST_TN)], weights_bf=ffn1_bf)
        p, ffn2_bf = _inproj(h, mods, norm_g, w_in_bf, l, cos, sin, ffn2_wi, ffn2_wo)
        n_rows = SEQ if last else ROWS
        attn, mix_bf = _attention(p, sinks, l, n_rows, mix_weights)
        h = _mix(h, p, attn, mods, l, pool_scale, conv_w, mix_bf, n_rows)
        next_ffn1 = [] if last else [(ffn1_wi, l + 1, 1, HOST_TN // 2), (ffn1_wo, l + 1, 0, HOST_TN // 4)]
        h, ffn1_bf = _ffn(h, FFN_TM_LATENT if last else FFN_TM, FFN_TF, mods, norm_g, ffn2_wi, ffn2_wo, l, 6,
                          final_g, final=last, hosted=next_ffn1, weights_bf=ffn2_bf)
    return h.reshape(1, SEQ, D_MODEL)
```

```python
import functools

import jax
import jax.numpy as jnp
import numpy as np
from jax import lax
from jax.experimental import pallas as pl
from jax.experimental.pallas import tpu as pltpu

D_MODEL = 2048
SEQ = 8192
CTX_LEN = 256
ROWS = SEQ + CTX_LEN
DEPTH = 2
GRID_W = 64
HEAD_DIM = 128
N_Q_HEADS = 16
N_KV_HEADS = 4
GQA_GROUP = 4
WINDOW = 128
BLOCK = 128
ROPE_PAIRS = 32
ROPE_BASE = 10000.0
POOL_WINDOWS = (2, 4, 8, 16)
POOL_WIDTH = 1024
POOL_GROUP = 256
CONV_WIDTH = 1024
D_FF = 5632
N_MODS = 9
EPS = 1e-6
NEG_INF = -1e30
LOG2E = 1.4426950408889634

K_OFF = 2048
V_OFF = 2560
POOL_OFF = 3072
CB_OFF = 4096
CC_OFF = 5120
CX_OFF = 6144
GATE_OFF = 7168
IN_COLS = GATE_OFF + 3 * D_MODEL

BF16 = jnp.bfloat16
F32 = jnp.float32

VMEM_LIMIT_BYTES = 62 * 1024 * 1024

FFN_TM = 1056
FFN_TM_LATENT = 1024
PROJ_TM = 1408
FFN_TF = 512
HOST_TN = 256
FFN_TF_CAST = 256
PROJ_TN = 1024
PROJ_CHUNK = 512
ROW_CHUNK = 32
ROWS_UNROLL = 4
HIDDEN_CHUNK = 256
COL_CHUNK = 512
MIX_TM = 256
MIX_WEIGHT_ROWS = (2048, 1024, 1024, 2048, 1024)
GATE_TN = 1024
HALO = 16
POOL_K = 384


def _cparams(sem):
    return pltpu.CompilerParams(dimension_semantics=sem, vmem_limit_bytes=VMEM_LIMIT_BYTES)


def _layer_vector_specs(layer, norm_idx):
    once = pl.Buffered(1)
    return [pl.BlockSpec((None, 8, N_MODS * D_MODEL), lambda *_: (layer, 0, 0), pipeline_mode=once),
            pl.BlockSpec((None, 1, D_MODEL), lambda *_: (layer * 3 + norm_idx, 0, 0), pipeline_mode=once)]


def _hosted_cast(w, layer, axis, size, step, n_steps):
    rows, cols = w.shape[1:]
    n_slabs = (rows, cols)[axis] // size
    assert n_slabs * size == (rows, cols)[axis] and n_slabs <= n_steps
    block = (size, cols) if axis == 0 else (rows, size)

    def index(*ids):
        k = jnp.minimum(step(*ids), n_slabs - 1)
        return (k, 0) if axis == 0 else (0, k)

    return (pl.BlockSpec((None,) + block, lambda *ids: (layer,) + index(*ids)),
            pl.BlockSpec(block, index),
            jax.ShapeDtypeStruct((rows, cols), BF16))


def _dot(a, b):
    return jnp.dot(a, b, preferred_element_type=F32)


def _norm_mod(x, g, shift, scale):
    y = x * lax.rsqrt(jnp.mean(x * x, axis=-1, keepdims=True) + EPS)
    return y * (g * (1.0 + scale)) + shift


def _mod_rows(m_ref, k, is_ctx):
    lo, hi = k * D_MODEL, (k + 1) * D_MODEL
    return jnp.where(is_ctx, m_ref[1:2, lo:hi], m_ref[0:1, lo:hi])


assert SEQ % ROW_CHUNK == 0


def _is_ctx_rows(first_row):
    return first_row >= SEQ


def _rows_loop(tm, body, unroll):
    def step(r, carry):
        body(pl.multiple_of(r * ROW_CHUNK, ROW_CHUNK))
        return carry
    lax.fori_loop(0, tm // ROW_CHUNK, step, 0, unroll=unroll)


def _norm_mod_rows(x_ref, xm_ref, g_ref, m_ref, k_shift, row0, tm):
    def rows(r0):
        rs = pl.ds(r0, ROW_CHUNK)
        is_ctx = _is_ctx_rows(row0 + r0)
        xm_ref[rs, :] = _norm_mod(x_ref[rs, :], g_ref[...], _mod_rows(m_ref, k_shift, is_ctx),
                                  _mod_rows(m_ref, k_shift + 1, is_ctx)).astype(BF16)
    _rows_loop(tm, rows, ROWS_UNROLL)


def _ada_kernel(c_ref, w_ref, b_ref, o_ref):
    c = c_ref[...]
    s = c * jax.nn.sigmoid(c)
    o_ref[...] = _dot(s.astype(BF16), w_ref[...].astype(BF16)) + b_ref[...]


def _ada(cond, w_ada, b_ada):
    tn = 1024
    n_cols = N_MODS * D_MODEL
    return pl.pallas_call(
        _ada_kernel,
        grid=(DEPTH, n_cols // tn),
        in_specs=[
            pl.BlockSpec((8, D_MODEL), lambda l, j: (0, 0)),
            pl.BlockSpec((None, D_MODEL, tn), lambda l, j: (l, 0, j)),
            pl.BlockSpec((None, 1, tn), lambda l, j: (l, 0, j)),
        ],
        out_specs=pl.BlockSpec((None, 8, tn), lambda l, j: (l, 0, j)),
        out_shape=jax.ShapeDtypeStruct((DEPTH, 8, n_cols), F32),
        compiler_params=_cparams(("arbitrary", "arbitrary")),
        name="ada",
    )(cond, w_ada, b_ada.reshape(DEPTH, 1, n_cols))


def _ffn_kernel(*refs, tm, tile0, k0, final, cast, host):
    if cast:
        (x_ref, m_ref, g_ref, wig_in, wiu_in, wo_in, fg_ref,
         o_ref, wig_ref, wiu_ref, wo_ref, xm_ref) = refs
    else:
        x_ref, m_ref, g_ref, wig_ref, wiu_ref, wo_ref, fg_ref = refs[:7]
        o_ref, xm_ref = refs[7 + host], refs[-1]
        for src, dst in zip(refs[7:7 + host], refs[8 + host:-1]):
            dst[...] = src[...].astype(BF16)
    row0 = (pl.program_id(0) + tile0) * tm
    f = pl.program_id(1)

    @pl.when(f == 0)
    def _():
        _norm_mod_rows(x_ref, xm_ref, g_ref, m_ref, k0, row0, tm)
        o_ref[...] = jnp.zeros_like(o_ref)

    if cast:
        wig_ref[...] = wig_in[...].astype(BF16)
        wiu_ref[...] = wiu_in[...].astype(BF16)
        wo_ref[...] = wo_in[...].astype(BF16)
    xm = xm_ref[...]
    parts = []
    for h0 in range(0, wig_ref.shape[1], HIDDEN_CHUNK):
        hs = slice(h0, h0 + HIDDEN_CHUNK)
        g = _dot(xm, wig_ref[:, hs])
        u = _dot(xm, wiu_ref[:, hs])
        parts.append(((g * jax.nn.sigmoid(g)) * u).astype(BF16))
    a = parts[0] if len(parts) == 1 else jnp.concatenate(parts, axis=1)
    for c0 in range(0, D_MODEL, COL_CHUNK):
        o_ref[:, c0:c0 + COL_CHUNK] += _dot(a, wo_ref[:, c0:c0 + COL_CHUNK])

    @pl.when(f == pl.num_programs(1) - 1)
    def _():
        def rows(r0):
            rs = pl.ds(r0, ROW_CHUNK)
            gate = _mod_rows(m_ref, k0 + 2, _is_ctx_rows(row0 + r0))
            y = x_ref[rs, :] + (0.5 * gate) * o_ref[rs, :]
            if final:
                y = (y * lax.rsqrt(jnp.mean(y * y, axis=-1, keepdims=True) + EPS)) * fg_ref[...]
            o_ref[rs, :] = y
        if final:
            for r0 in range(0, tm, ROW_CHUNK):
                rows(r0)
        else:
            _rows_loop(tm, rows, 1)


def _ffn(x, tm, tf, mods, norm_g, wi, wo, layer, k0, final_g, final=False, hosted=(), weights_bf=None):
    n_rows = x.shape[0]
    kern = functools.partial(_ffn_kernel, tm=tm, k0=k0, final=final)
    vec = lambda i, f: (0, 0)
    small = _layer_vector_specs(layer, k0 // 3)
    scratch = [pltpu.VMEM((tm, D_MODEL), BF16)]
    g2, fg2 = norm_g.reshape(DEPTH * 3, 1, D_MODEL), final_g.reshape(1, D_MODEL)
    n_f = D_FF // tf
    if weights_bf is not None:
        wig = wiu = weights_bf[0]
        wo_bf, up_block0, tile0 = weights_bf[1], n_f, 0
    else:
        x, wig, wiu, wo_bf = _ffn_cast_tile(x, kern, small, scratch, mods, g2, fg2, wi, wo, layer, tm, final)
        up_block0, tile0 = 0, 1

    n_i = n_rows // tm - tile0
    in_specs = [pl.BlockSpec((tm, D_MODEL), lambda i, f: (i + tile0, 0))] + small + [
        pl.BlockSpec((D_MODEL, tf), lambda i, f: (0, f)),
        pl.BlockSpec((D_MODEL, tf), lambda i, f: (0, up_block0 + f)),
        pl.BlockSpec((tf, D_MODEL), lambda i, f: (f, 0)),
        pl.BlockSpec((1, D_MODEL), vec)]
    out_specs = [pl.BlockSpec((tm, D_MODEL), lambda i, f: (i + tile0, 0))]
    out_shape = [jax.ShapeDtypeStruct((n_rows, D_MODEL), F32)]
    operands = [x, mods, g2, wig, wiu, wo_bf, fg2]
    for w, w_layer, axis, size in hosted:
        spec_in, spec_out, shape = _hosted_cast(w, w_layer, axis, size, lambda i, f: i * n_f + f, n_i * n_f)
        in_specs.append(spec_in)
        out_specs.append(spec_out)
        out_shape.append(shape)
        operands.append(w)
    outs = pl.pallas_call(
        functools.partial(kern, tile0=tile0, cast=False, host=len(hosted)),
        grid=(n_i, n_f),
        in_specs=in_specs,
        out_specs=out_specs,
        out_shape=out_shape,
        input_output_aliases={0: 0},
        scratch_shapes=scratch,
        compiler_params=_cparams(("arbitrary", "arbitrary")),
        name="ffn_final" if final else "ffn",
    )(*operands)
    return outs[0], tuple(outs[1:])


def _ffn_cast_tile(x, kern, small, scratch, mods, g2, fg2, wi, wo, layer, tm, final):
    n_rows = x.shape[0]
    vec = lambda i, f: (0, 0)
    tc = FFN_TF_CAST
    n_fc = D_FF // tc
    return pl.pallas_call(
        functools.partial(kern, tile0=0, cast=True, host=0),
        grid=(1, n_fc),
        in_specs=[pl.BlockSpec((tm, D_MODEL), lambda i, f: (0, 0), pipeline_mode=pl.Buffered(1))] + small + [
            pl.BlockSpec((None, D_MODEL, tc), lambda i, f: (layer, 0, f)),
            pl.BlockSpec((None, D_MODEL, tc), lambda i, f: (layer, 0, n_fc + f)),
            pl.BlockSpec((None, tc, D_MODEL), lambda i, f: (layer, f, 0)),
            pl.BlockSpec((1, D_MODEL), vec)],
        out_specs=[pl.BlockSpec((tm, D_MODEL), lambda i, f: (0, 0)),
                   pl.BlockSpec((D_MODEL, tc), lambda i, f: (0, f)),
                   pl.BlockSpec((D_MODEL, tc), lambda i, f: (0, f)),
                   pl.BlockSpec((tc, D_MODEL), lambda i, f: (f, 0))],
        out_shape=[jax.ShapeDtypeStruct((n_rows, D_MODEL), F32),
                   jax.ShapeDtypeStruct((D_MODEL, D_FF), BF16),
                   jax.ShapeDtypeStruct((D_MODEL, D_FF), BF16),
                   jax.ShapeDtypeStruct((D_FF, D_MODEL), BF16)],
        input_output_aliases={0: 0},
        scratch_shapes=scratch,
        compiler_params=_cparams(("arbitrary", "arbitrary")),
        name="ffn_cast_final" if final else "ffn_cast",
    )(x, mods, g2, wi, wi, wo, fg2)


def _rope(t, cos, sin):
    lane = lax.broadcasted_iota(jnp.int32, (1, HEAD_DIM), 1)
    first_half = (lane % (2 * ROPE_PAIRS)) < ROPE_PAIRS
    outs = []
    for hd in range(t.shape[1] // HEAD_DIM):
        th = t[:, hd * HEAD_DIM:(hd + 1) * HEAD_DIM]
        partner = jnp.where(first_half, pltpu.roll(th, HEAD_DIM - ROPE_PAIRS, 1),
                            pltpu.roll(th, ROPE_PAIRS, 1))
        outs.append(th * cos + partner * sin)
    return jnp.concatenate(outs, axis=1)


def _inproj_kernel(*refs, tm, host):
    x_ref, m_ref, g_ref, w_ref, cos_ref, sin_ref = refs[:6]
    o_ref, xn_ref = refs[6 + host], refs[-1]
    j = pl.program_id(1)
    q_blocks = K_OFF // PROJ_TN
    plain_end = GATE_OFF // PROJ_TN

    def project(epilogue):
        for src, dst in zip(refs[6:6 + host], refs[7 + host:-1]):
            dst[...] = src[...].astype(BF16)
        for c0 in range(0, PROJ_TN, PROJ_CHUNK):
            cs = slice(c0, c0 + PROJ_CHUNK)
            o_ref[:, cs] = epilogue(_dot(xn_ref[...], w_ref[:, cs]), c0).astype(BF16)

    def rope(acc):
        return _rope(acc, cos_ref[...], sin_ref[...])

    @pl.when(j == 0)
    def _():
        _norm_mod_rows(x_ref, xn_ref, g_ref, m_ref, 3, pl.program_id(0) * tm, tm)

    @pl.when(j < q_blocks)
    def _():
        project(lambda acc, c0: rope(acc) * (HEAD_DIM ** -0.5 * LOG2E))

    @pl.when(j == q_blocks)
    def _():
        project(lambda acc, c0: rope(acc) if c0 < V_OFF - K_OFF else acc)

    @pl.when((j > q_blocks) & (j < plain_end))
    def _():
        project(lambda acc, c0: acc)

    @pl.when(j >= plain_end)
    def _():
        project(lambda acc, c0: 0.5 * jnp.tanh(0.5 * acc) + 0.5)


def _inproj(x, mods, norm_g, w_bf, layer, cos, sin, ffn_wi, ffn_wo):
    tm = PROJ_TM
    n_i, n_j = ROWS // tm, IN_COLS // PROJ_TN
    row = lambda width: pl.BlockSpec((tm, width), lambda i, j: (i, 0))
    step = lambda i, j: i * n_j + j
    hosted = [_hosted_cast(ffn_wi, layer, 1, HOST_TN, step, n_i * n_j),
              _hosted_cast(ffn_wo, layer, 0, HOST_TN // 2, step, n_i * n_j)]
    outs = pl.pallas_call(
        functools.partial(_inproj_kernel, tm=tm, host=len(hosted)),
        grid=(n_i, n_j),
        in_specs=[row(D_MODEL)] + _layer_vector_specs(layer, 1) + [
            pl.BlockSpec((D_MODEL, PROJ_TN), lambda i, j: (0, j)),
            row(HEAD_DIM), row(HEAD_DIM)] + [h[0] for h in hosted],
        out_specs=[pl.BlockSpec((tm, PROJ_TN), lambda i, j: (i, j))] + [h[1] for h in hosted],
        out_shape=[jax.ShapeDtypeStruct((ROWS, IN_COLS), BF16)] + [h[2] for h in hosted],
        scratch_shapes=[pltpu.VMEM((tm, D_MODEL), BF16)],
        compiler_params=_cparams(("arbitrary", "arbitrary")),
        name="inproj",
    )(x, mods, norm_g.reshape(DEPTH * 3, 1, D_MODEL), w_bf, cos, sin, ffn_wi, ffn_wo)
    return outs[0], (outs[1], outs[2])


def _rope_tables():
    pos = np.arange(SEQ)
    inv = ROPE_BASE ** (-np.arange(ROPE_PAIRS, dtype=np.float64) / ROPE_PAIRS)
    ang_r = (pos // GRID_W)[:, None] * inv
    ang_c = (pos % GRID_W)[:, None] * inv
    cos = np.concatenate([np.cos(ang_r), np.cos(ang_r), np.cos(ang_c), np.cos(ang_c)], axis=1)
    sin = np.concatenate([-np.sin(ang_r), np.sin(ang_r), -np.sin(ang_c), np.sin(ang_c)], axis=1)
    cos = np.concatenate([cos, np.ones((CTX_LEN, HEAD_DIM))], axis=0)
    sin = np.concatenate([sin, np.zeros((CTX_LEN, HEAD_DIM))], axis=0)
    return jnp.asarray(cos, F32), jnp.asarray(sin, F32)


def _attn_kernel(sink_ref, q_ref, kp_ref, kc_ref, kn_ref, vp_ref, vc_ref, vn_ref, kx_ref, vx_ref,
                 *rest, layer):
    n_w = len(MIX_WEIGHT_ROWS)
    w_in, o_ref, w_out = rest[:n_w], rest[n_w], rest[n_w + 1:]
    for src, dst in zip(w_in, w_out):
        dst[...] = src[...].astype(BF16)
    n = pl.program_id(0)
    n_lat = SEQ // BLOCK
    r = lax.broadcasted_iota(jnp.int32, (BLOCK, BLOCK), 0)
    c = lax.broadcasted_iota(jnp.int32, (BLOCK, BLOCK), 1)
    is_lat = n < n_lat
    prev_ok = (c >= r) & (n >= 1) & is_lat
    next_ok = (c <= r) & (n < n_lat - 1)
    n_keys = 3 * BLOCK + CTX_LEN
    ones = jnp.ones((n_keys, HEAD_DIM), BF16)
    contract_last = (((1,), (1,)), ((), ()))
    for h in range(N_KV_HEADS):
        hs = slice(h * HEAD_DIM, (h + 1) * HEAD_DIM)
        qg = jnp.concatenate(
            [q_ref[:, (h * GQA_GROUP + g) * HEAD_DIM:(h * GQA_GROUP + g + 1) * HEAD_DIM]
             for g in range(GQA_GROUP)], axis=0)
        kb = jnp.concatenate([kp_ref[:, hs], kc_ref[:, hs], kn_ref[:, hs], kx_ref[:, hs]], axis=0)
        vb = jnp.concatenate([vp_ref[:, hs], vc_ref[:, hs], vn_ref[:, hs], vx_ref[:, hs]], axis=0)
        vb = jnp.concatenate([vb, ones], axis=1)
        s = lax.dot_general(qg, kb, contract_last, preferred_element_type=F32)
        probs, maxes, sinks = [], [], []
        for g in range(GQA_GROUP):
            sg = s[g * BLOCK:(g + 1) * BLOCK]
            parts = [jnp.where(prev_ok, sg[:, 0:BLOCK], NEG_INF),
                     jnp.where(is_lat, sg[:, BLOCK:2 * BLOCK], NEG_INF),
                     jnp.where(next_ok, sg[:, 2 * BLOCK:3 * BLOCK], NEG_INF),
                     sg[:, 3 * BLOCK:4 * BLOCK], sg[:, 4 * BLOCK:5 * BLOCK]]
            sink = sink_ref[layer * N_Q_HEADS + h * GQA_GROUP + g] * LOG2E
            lane_max = jnp.maximum(jnp.maximum(jnp.maximum(parts[0], parts[1]),
                                               jnp.maximum(parts[2], parts[3])), parts[4])
            m = jnp.maximum(jnp.max(lane_max, axis=-1, keepdims=True), sink)
            probs.append(jnp.concatenate([jnp.exp2(pt - m).astype(BF16) for pt in parts], axis=1))
            maxes.append(m)
            sinks.append(sink)
        oe = _dot(jnp.concatenate(probs, axis=0), vb)
        for g in range(GQA_GROUP):
            hq = h * GQA_GROUP + g
            og = oe[g * BLOCK:(g + 1) * BLOCK]
            denom = og[:, HEAD_DIM:HEAD_DIM + 1] + jnp.exp2(sinks[g] - maxes[g])
            o_ref[:, hq * HEAD_DIM:(hq + 1) * HEAD_DIM] = (og[:, :HEAD_DIM] / denom).astype(BF16)


def _attention(p, sinks, layer, n_rows, mix_weights):
    n_blocks = n_rows // BLOCK
    last = ROWS // BLOCK - 1
    kcol, vcol = K_OFF // 512, V_OFF // 512
    ctx_blk = SEQ // CTX_LEN
    kv = lambda col, off: pl.BlockSpec(
        (BLOCK, 512), lambda n, *_: (jnp.clip(n + off, 0, last), col))
    n_slabs = SEQ // BLOCK
    slab = lambda n: jnp.minimum(n, n_slabs - 1)
    w_in_specs, w_out_specs, w_out_shapes = [], [], []
    for w, rows in zip(mix_weights, MIX_WEIGHT_ROWS):
        assert w.shape[1] == rows
        sr, width = rows // n_slabs, w.shape[2]
        w_in_specs.append(pl.BlockSpec((None, sr, width), lambda n, *_: (layer, slab(n), 0)))
        w_out_specs.append(pl.BlockSpec((sr, width), lambda n, *_: (slab(n), 0)))
        w_out_shapes.append(jax.ShapeDtypeStruct((rows, width), BF16))
    outs = pl.pallas_call(
        functools.partial(_attn_kernel, layer=layer),
        grid_spec=pltpu.PrefetchScalarGridSpec(
            num_scalar_prefetch=1,
            grid=(n_blocks,),
            in_specs=[
                pl.BlockSpec((BLOCK, D_MODEL), lambda n, *_: (n, 0)),
                kv(kcol, -1), kv(kcol, 0), kv(kcol, 1),
                kv(vcol, -1), kv(vcol, 0), kv(vcol, 1),
                pl.BlockSpec((CTX_LEN, 512), lambda n, *_: (ctx_blk, kcol)),
                pl.BlockSpec((CTX_LEN, 512), lambda n, *_: (ctx_blk, vcol)),
            ] + w_in_specs,
            out_specs=[pl.BlockSpec((BLOCK, D_MODEL), lambda n, *_: (n, 0))] + w_out_specs,
        ),
        out_shape=[jax.ShapeDtypeStruct((n_rows, D_MODEL), BF16)] + w_out_shapes,
        compiler_params=_cparams(("arbitrary",)),
        name="attention",
    )(sinks, p, p, p, p, p, p, p, p, p, *mix_weights)
    return outs[0], outs[1:]


def _mix_kernel(x_ref, a_ref, u_ref, up_ref, un_ref, cb_ref, cc_ref, ccp_ref, ccn_ref,
                cx_ref, cxp_ref, cxn_ref, g0a_ref, g0b_ref, g1a_ref, g1b_ref, g2a_ref, g2b_ref,
                m_ref, pw_ref, ps_ref, cw_ref, wa_ref, wp_ref, wc_ref, wo_ref, o_ref, ub_ref):
    t = pl.program_id(0)
    tm = MIX_TM
    lat_tiles = SEQ // tm
    is_ctx = t >= lat_tiles
    seg_first = (t == 0) | is_ctx
    seg_last = (t == lat_tiles - 1) | is_ctx
    seg_len = jnp.where(is_ctx, CTX_LEN, SEQ)
    row = lax.broadcasted_iota(jnp.int32, (tm, 1), 0)
    pos = jnp.where(is_ctx, 0, t * tm) + row

    y_attn = _dot(a_ref[...], wa_ref[...])

    ub_ref[0:tm, :] = u_ref[...]
    ub_ref[tm:tm + HALO, :] = jnp.where(seg_last, jnp.zeros_like(un_ref[...]), un_ref[...])
    ub_ref[tm + HALO:POOL_K - HALO, :] = jnp.zeros((POOL_K - 2 * HALO - tm, POOL_WIDTH), BF16)
    ub_ref[POOL_K - HALO:POOL_K, :] = jnp.where(seg_first, jnp.zeros_like(up_ref[...]), up_ref[...])
    br = lax.broadcasted_iota(jnp.int32, (tm, POOL_K), 0)
    bc = lax.broadcasted_iota(jnp.int32, (tm, POOL_K), 1)
    d = bc - br
    d = jnp.where(d >= POOL_K // 2, d - POOL_K, d)
    pooled = []
    for gi, w in enumerate(POOL_WINDOWS):
        lo, hi = w // 2, w - w // 2 - 1
        gs = slice(gi * POOL_GROUP, (gi + 1) * POOL_GROUP)
        band = ((d >= -lo) & (d <= hi)).astype(BF16)
        wsum = _dot(band, ub_ref[:, gs])
        cnt = (jnp.minimum(pos + hi, seg_len - 1) - jnp.maximum(pos - lo, 0) + 1).astype(F32)
        y = wsum / cnt - u_ref[:, gs].astype(F32)
        pooled.append(_dot(y.astype(BF16), pw_ref[gs, :]))
    y_pool = jnp.concatenate(pooled, axis=1) * ps_ref[...]
    y_pool = _dot(y_pool.astype(BF16), wp_ref[...])

    z = cc_ref[...].astype(F32) * cx_ref[...].astype(F32)
    z_prev = ccp_ref[HALO - 1:HALO, :].astype(F32) * cxp_ref[HALO - 1:HALO, :].astype(F32)
    z_next = ccn_ref[0:1, :].astype(F32) * cxn_ref[0:1, :].astype(F32)
    z_prev = jnp.where(seg_first, 0.0, z_prev)
    z_next = jnp.where(seg_last, 0.0, z_next)
    zm1 = jnp.where(row == 0, z_prev, pltpu.roll(z, 1, 0))
    zp1 = jnp.where(row == tm - 1, z_next, pltpu.roll(z, tm - 1, 0))
    conv = zm1 * cw_ref[0:1, :] + z * cw_ref[1:2, :] + zp1 * cw_ref[2:3, :]
    y_conv = _dot((cb_ref[...].astype(F32) * conv).astype(BF16), wc_ref[...])

    halves = []
    for k, (g0, g1, g2) in enumerate(((g0a_ref, g1a_ref, g2a_ref), (g0b_ref, g1b_ref, g2b_ref))):
        cs = slice(k * GATE_TN, (k + 1) * GATE_TN)
        halves.append((g0[...].astype(F32) * y_attn[:, cs] + g1[...].astype(F32) * y_pool[:, cs]
                       + g2[...].astype(F32) * y_conv[:, cs]).astype(BF16))
    out = _dot(jnp.concatenate(halves, axis=1), wo_ref[...])
    o_ref[...] = x_ref[...] + _mod_rows(m_ref, 5, is_ctx) * out


def _mix(x, p, attn, mods, layer, pool_scale, conv_w, weights_bf, n_rows):
    w_attn_out, w_pool_out, w_conv_out, w_o, pool_w = weights_bf
    tm = MIX_TM
    n_tiles = n_rows // tm
    hb = tm // HALO
    last_halo = ROWS // HALO - 1
    tile = lambda width, col: pl.BlockSpec((tm, width), lambda t: (t, col))
    prev = lambda col: pl.BlockSpec((HALO, POOL_WIDTH), lambda t: (jnp.maximum(t * hb - 1, 0), col))
    nxt = lambda col: pl.BlockSpec((HALO, POOL_WIDTH), lambda t: (jnp.minimum((t + 1) * hb, last_halo), col))
    const = lambda shape: pl.BlockSpec(shape, lambda t: (0,) * len(shape), pipeline_mode=pl.Buffered(1))
    ucol, cbcol, cccol, cxcol = POOL_OFF // 1024, CB_OFF // 1024, CC_OFF // 1024, CX_OFF // 1024
    gcol = GATE_OFF // GATE_TN
    return pl.pallas_call(
        _mix_kernel,
        grid=(n_tiles,),
        in_specs=[
            tile(D_MODEL, 0),
            tile(D_MODEL, 0),
            tile(1024, ucol), prev(ucol), nxt(ucol),
            tile(1024, cbcol),
            tile(1024, cccol), prev(cccol), nxt(cccol),
            tile(1024, cxcol), prev(cxcol), nxt(cxcol),
            *[tile(GATE_TN, gcol + k) for k in range(3 * D_MODEL // GATE_TN)],
            pl.BlockSpec((None, 8, N_MODS * D_MODEL), lambda t: (layer, 0, 0), pipeline_mode=pl.Buffered(1)),
            const((POOL_WIDTH, POOL_GROUP)),
            pl.BlockSpec((None, 1, POOL_WIDTH), lambda t: (layer, 0, 0), pipeline_mode=pl.Buffered(1)),
            pl.BlockSpec((None, 3, CONV_WIDTH), lambda t: (layer, 0, 0), pipeline_mode=pl.Buffered(1)),
            const((D_MODEL, D_MODEL)),
            const((POOL_WIDTH, D_MODEL)),
            const((CONV_WIDTH, D_MODEL)),
            const((D_MODEL, D_MODEL)),
        ],
        out_specs=pl.BlockSpec((tm, D_MODEL), lambda t: (t, 0)),
        out_shape=jax.ShapeDtypeStruct((n_rows, D_MODEL), F32),
        scratch_shapes=[pltpu.VMEM((POOL_K, POOL_WIDTH), BF16)],
        compiler_params=_cparams(("arbitrary",)),
        name="mix",
    )(x, attn, *([p] * 16), mods, pool_w, pool_scale.reshape(DEPTH, 1, POOL_WIDTH),
      conv_w, w_attn_out, w_pool_out, w_conv_out, w_o)


def kernel(x, c, ctx, c_ctx, w_ada, b_ada, norm_g, ffn1_wi, ffn1_wo, w_in, attn_sink, pool_w, pool_scale,
           conv_w, w_attn_out, w_pool_out, w_conv_out, w_o, ffn2_wi, ffn2_wo, final_g):
    assert x.shape == (1, SEQ, D_MODEL) and ctx.shape == (1, CTX_LEN, D_MODEL)
    cond = jnp.concatenate([c.reshape(1, D_MODEL), c_ctx.reshape(1, D_MODEL),
                            jnp.zeros((6, D_MODEL), F32)], axis=0)
    mods = _ada(cond, w_ada, b_ada)
    cos, sin = _rope_tables()
    mix_weights = (w_attn_out, w_pool_out, w_conv_out, w_o,
                   pool_w.reshape(DEPTH, POOL_WIDTH, POOL_GROUP))
    sinks = attn_sink.reshape(DEPTH * N_Q_HEADS)

    h = jnp.concatenate([x[0], ctx[0]], axis=0)
    ffn1_bf = None
    for l in range(DEPTH):
        last = l == DEPTH - 1
        h, (w_in_bf,) = _ffn(h, FFN_TM, FFN_TF, mods, norm_g, ffn1_wi, ffn1_wo, l, 0, final_g,
                             hosted=[(w_in, l, 1, HOST_TN)], weights_bf=ffn1_bf)
        p, ffn2_bf = _inproj(h, mods, norm_g, w_in_bf, l, cos, sin, ffn2_wi, ffn2_wo)
        n_rows = SEQ if last else ROWS
        attn, mix_bf = _attention(p, sinks, l, n_rows, mix_weights)
        h = _mix(h, p, attn, mods, l, pool_scale, conv_w, mix_bf, n_rows)
        next_ffn1 = [] if last else [(ffn1_wi, l + 1, 1, HOST_TN // 2), (ffn1_wo, l + 1, 0, HOST_TN // 4)]
        h, ffn1_bf = _ffn(h, FFN_TM_LATENT if last else FFN_TM, FFN_TF, mods, norm_g, ffn2_wi, ffn2_wo, l, 6,
                          final_g, final=last, hosted=next_ffn1, weights_bf=ffn2_bf)
    return h.reshape(1, SEQ, D_MODEL)
```

```python
import functools

import jax
import jax.numpy as jnp
import numpy as np
from jax import lax
from jax.experimental import pallas as pl
from jax.experimental.pallas import tpu as pltpu

D_MODEL = 2048
SEQ = 8192
CTX_LEN = 256
ROWS = SEQ + CTX_LEN
DEPTH = 2
GRID_W = 64
HEAD_DIM = 128
N_Q_HEADS = 16
N_KV_HEADS = 4
GQA_GROUP = 4
WINDOW = 128
BLOCK = 128
ROPE_PAIRS = 32
ROPE_BASE = 10000.0
POOL_WINDOWS = (2, 4, 8, 16)
POOL_WIDTH = 1024
POOL_GROUP = 256
CONV_WIDTH = 1024
D_FF = 5632
N_MODS = 9
EPS = 1e-6
NEG_INF = -1e30
LOG2E = 1.4426950408889634

K_OFF = 2048
V_OFF = 2560
POOL_OFF = 3072
CB_OFF = 4096
CC_OFF = 5120
CX_OFF = 6144
GATE_OFF = 7168
IN_COLS = GATE_OFF + 3 * D_MODEL

BF16 = jnp.bfloat16
F32 = jnp.float32

VMEM_LIMIT_BYTES = 62 * 1024 * 1024

FFN_TM = 1056
FFN_TM_LATENT = 1024
PROJ_TM = 1408
FFN_TF = 512
HOST_TN = 256
FFN_TF_CAST = 256
PROJ_TN = 1024
PROJ_CHUNK = 512
ROW_CHUNK = 32
ROWS_UNROLL = 4
HIDDEN_CHUNK = 256
COL_CHUNK = 512
MIX_TM = 256
MIX_WEIGHT_ROWS = (2048, 1024, 1024, 2048, 1024)
GATE_TN = 1024
HALO = 16
POOL_K = 384


def _cparams(sem):
    return pltpu.CompilerParams(dimension_semantics=sem, vmem_limit_bytes=VMEM_LIMIT_BYTES)


def _layer_vector_specs(layer, norm_idx):
    once = pl.Buffered(1)
    return [pl.BlockSpec((None, 8, N_MODS * D_MODEL), lambda *_: (layer, 0, 0), pipeline_mode=once),
            pl.BlockSpec((None, 1, D_MODEL), lambda *_: (layer * 3 + norm_idx, 0, 0), pipeline_mode=once)]


def _hosted_cast(w, layer, axis, size, step, n_steps):
    rows, cols = w.shape[1:]
    n_slabs = (rows, cols)[axis] // size
    assert n_slabs * size == (rows, cols)[axis] and n_slabs <= n_steps
    block = (size, cols) if axis == 0 else (rows, size)

    def index(*ids):
        k = jnp.minimum(step(*ids), n_slabs - 1)
        return (k, 0) if axis == 0 else (0, k)

    return (pl.BlockSpec((None,) + block, lambda *ids: (layer,) + index(*ids)),
            pl.BlockSpec(block, index),
            jax.ShapeDtypeStruct((rows, cols), BF16))


def _dot(a, b):
    return jnp.dot(a, b, preferred_element_type=F32)


def _norm_mod(x, g, shift, scale):
    y = x * lax.rsqrt(jnp.mean(x * x, axis=-1, keepdims=True) + EPS)
    return y * (g * (1.0 + scale)) + shift


def _mod_rows(m_ref, k, is_ctx):
    lo, hi = k * D_MODEL, (k + 1) * D_MODEL
    return jnp.where(is_ctx, m_ref[1:2, lo:hi], m_ref[0:1, lo:hi])


assert SEQ % ROW_CHUNK == 0


def _is_ctx_rows(first_row):
    return first_row >= SEQ


def _rows_loop(tm, body, unroll):
    def step(r, carry):
        body(pl.multiple_of(r * ROW_CHUNK, ROW_CHUNK))
        return carry
    lax.fori_loop(0, tm // ROW_CHUNK, step, 0, unroll=unroll)


def _norm_mod_rows(x_ref, xm_ref, g_ref, m_ref, k_shift, row0, tm):
    def rows(r0):
        rs = pl.ds(r0, ROW_CHUNK)
        is_ctx = _is_ctx_rows(row0 + r0)
        xm_ref[rs, :] = _norm_mod(x_ref[rs, :], g_ref[...], _mod_rows(m_ref, k_shift, is_ctx),
                                  _mod_rows(m_ref, k_shift + 1, is_ctx)).astype(BF16)
    _rows_loop(tm, rows, ROWS_UNROLL)


def _ada_kernel(c_ref, w_ref, b_ref, o_ref):
    c = c_ref[...]
    s = c * jax.nn.sigmoid(c)
    o_ref[...] = _dot(s.astype(BF16), w_ref[...].astype(BF16)) + b_ref[...]


def _ada(cond, w_ada, b_ada):
    tn = 1024
    n_cols = N_MODS * D_MODEL
    return pl.pallas_call(
        _ada_kernel,
        grid=(DEPTH, n_cols // tn),
        in_specs=[
            pl.BlockSpec((8, D_MODEL), lambda l, j: (0, 0)),
            pl.BlockSpec((None, D_MODEL, tn), lambda l, j: (l, 0, j)),
            pl.BlockSpec((None, 1, tn), lambda l, j: (l, 0, j)),
        ],
        out_specs=pl.BlockSpec((None, 8, tn), lambda l, j: (l, 0, j)),
        out_shape=jax.ShapeDtypeStruct((DEPTH, 8, n_cols), F32),
        compiler_params=_cparams(("arbitrary", "arbitrary")),
        name="ada",
    )(cond, w_ada, b_ada.reshape(DEPTH, 1, n_cols))


def _ffn_kernel(*refs, tm, tile0, k0, final, cast, host):
    if cast:
        (x_ref, m_ref, g_ref, wig_in, wiu_in, wo_in, fg_ref,
         o_ref, wig_ref, wiu_ref, wo_ref, xm_ref) = refs
    else:
        x_ref, m_ref, g_ref, wig_ref, wiu_ref, wo_ref, fg_ref = refs[:7]
        o_ref, xm_ref = refs[7 + host], refs[-1]
    row0 = (pl.program_id(0) + tile0) * tm
    f = pl.program_id(1)
    last_f = pl.num_programs(1) - 1

    @pl.when(f == 0)
    def _():
        _norm_mod_rows(x_ref, xm_ref, g_ref, m_ref, k0, row0, tm)
        o_ref[...] = jnp.zeros_like(o_ref)

    def hidden_block(finish):
        if cast:
            wig_ref[...] = wig_in[...].astype(BF16)
            wiu_ref[...] = wiu_in[...].astype(BF16)
            wo_ref[...] = wo_in[...].astype(BF16)
        else:
            for src, dst in zip(refs[7:7 + host], refs[8 + host:-1]):
                dst[...] = src[...].astype(BF16)
        xm = xm_ref[...]
        parts = []
        for h0 in range(0, wig_ref.shape[1], HIDDEN_CHUNK):
            hs = slice(h0, h0 + HIDDEN_CHUNK)
            g = _dot(xm, wig_ref[:, hs])
            u = _dot(xm, wiu_ref[:, hs])
            parts.append(((g * jax.nn.sigmoid(g)) * u).astype(BF16))
        a = parts[0] if len(parts) == 1 else jnp.concatenate(parts, axis=1)
        if finish:
            is_ctx = (row0 + lax.broadcasted_iota(jnp.int32, (tm, 1), 0)) >= SEQ
            lo = (k0 + 2) * D_MODEL
        for c0 in range(0, D_MODEL, COL_CHUNK):
            cs = slice(c0, c0 + COL_CHUNK)
            acc = o_ref[:, cs] + _dot(a, wo_ref[:, cs])
            if finish:
                gate = jnp.where(is_ctx, m_ref[1:2, lo + c0:lo + c0 + COL_CHUNK],
                                 m_ref[0:1, lo + c0:lo + c0 + COL_CHUNK])
                acc = x_ref[:, cs] + (0.5 * gate) * acc
            o_ref[:, cs] = acc

    if final:
        hidden_block(False)

        @pl.when(f == last_f)
        def _():
            for r0 in range(0, tm, ROW_CHUNK):
                rs = pl.ds(r0, ROW_CHUNK)
                gate = _mod_rows(m_ref, k0 + 2, _is_ctx_rows(row0 + r0))
                y = x_ref[rs, :] + (0.5 * gate) * o_ref[rs, :]
                o_ref[rs, :] = (y * lax.rsqrt(jnp.mean(y * y, axis=-1, keepdims=True) + EPS)) * fg_ref[...]
    else:
        pl.when(f < last_f)(functools.partial(hidden_block, False))
        pl.when(f == last_f)(functools.partial(hidden_block, True))


def _ffn(x, tm, tf, mods, norm_g, wi, wo, layer, k0, final_g, final=False, hosted=(), weights_bf=None):
    n_rows = x.shape[0]
    kern = functools.partial(_ffn_kernel, tm=tm, k0=k0, final=final)
    vec = lambda i, f: (0, 0)
    small = _layer_vector_specs(layer, k0 // 3)
    scratch = [pltpu.VMEM((tm, D_MODEL), BF16)]
    g2, fg2 = norm_g.reshape(DEPTH * 3, 1, D_MODEL), final_g.reshape(1, D_MODEL)
    n_f = D_FF // tf
    if weights_bf is not None:
        wig = wiu = weights_bf[0]
        wo_bf, up_block0, tile0 = weights_bf[1], n_f, 0
    else:
        x, wig, wiu, wo_bf = _ffn_cast_tile(x, kern, small, scratch, mods, g2, fg2, wi, wo, layer, tm, final)
        up_block0, tile0 = 0, 1

    n_i = n_rows // tm - tile0
    in_specs = [pl.BlockSpec((tm, D_MODEL), lambda i, f: (i + tile0, 0))] + small + [
        pl.BlockSpec((D_MODEL, tf), lambda i, f: (0, f)),
        pl.BlockSpec((D_MODEL, tf), lambda i, f: (0, up_block0 + f)),
        pl.BlockSpec((tf, D_MODEL), lambda i, f: (f, 0)),
        pl.BlockSpec((1, D_MODEL), vec)]
    out_specs = [pl.BlockSpec((tm, D_MODEL), lambda i, f: (i + tile0, 0))]
    out_shape = [jax.ShapeDtypeStruct((n_rows, D_MODEL), F32)]
    operands = [x, mods, g2, wig, wiu, wo_bf, fg2]
    for w, w_layer, axis, size in hosted:
        spec_in, spec_out, shape = _hosted_cast(w, w_layer, axis, size, lambda i, f: i * n_f + f, n_i * n_f)
        in_specs.append(spec_in)
        out_specs.append(spec_out)
        out_shape.append(shape)
        operands.append(w)
    outs = pl.pallas_call(
        functools.partial(kern, tile0=tile0, cast=False, host=len(hosted)),
        grid=(n_i, n_f),
        in_specs=in_specs,
        out_specs=out_specs,
        out_shape=out_shape,
        input_output_aliases={0: 0},
        scratch_shapes=scratch,
        compiler_params=_cparams(("arbitrary", "arbitrary")),
        name="ffn_final" if final else "ffn",
    )(*operands)
    return outs[0], tuple(outs[1:])


def _ffn_cast_tile(x, kern, small, scratch, mods, g2, fg2, wi, wo, layer, tm, final):
    n_rows = x.shape[0]
    vec = lambda i, f: (0, 0)
    tc = FFN_TF_CAST
    n_fc = D_FF // tc
    return pl.pallas_call(
        functools.partial(kern, tile0=0, cast=True, host=0),
        grid=(1, n_fc),
        in_specs=[pl.BlockSpec((tm, D_MODEL), lambda i, f: (0, 0), pipeline_mode=pl.Buffered(1))] + small + [
            pl.BlockSpec((None, D_MODEL, tc), lambda i, f: (layer, 0, f)),
            pl.BlockSpec((None, D_MODEL, tc), lambda i, f: (layer, 0, n_fc + f)),
            pl.BlockSpec((None, tc, D_MODEL), lambda i, f: (layer, f, 0)),
            pl.BlockSpec((1, D_MODEL), vec)],
        out_specs=[pl.BlockSpec((tm, D_MODEL), lambda i, f: (0, 0)),
                   pl.BlockSpec((D_MODEL, tc), lambda i, f: (0, f)),
                   pl.BlockSpec((D_MODEL, tc), lambda i, f: (0, f)),
                   pl.BlockSpec((tc, D_MODEL), lambda i, f: (f, 0))],
        out_shape=[jax.ShapeDtypeStruct((n_rows, D_MODEL), F32),
                   jax.ShapeDtypeStruct((D_MODEL, D_FF), BF16),
                   jax.ShapeDtypeStruct((D_MODEL, D_FF), BF16),
                   jax.ShapeDtypeStruct((D_FF, D_MODEL), BF16)],
        input_output_aliases={0: 0},
        scratch_shapes=scratch,
        compiler_params=_cparams(("arbitrary", "arbitrary")),
        name="ffn_cast_final" if final else "ffn_cast",
    )(x, mods, g2, wi, wi, wo, fg2)


def _rope(t, cos, sin):
    lane = lax.broadcasted_iota(jnp.int32, (1, HEAD_DIM), 1)
    first_half = (lane % (2 * ROPE_PAIRS)) < ROPE_PAIRS
    outs = []
    for hd in range(t.shape[1] // HEAD_DIM):
        th = t[:, hd * HEAD_DIM:(hd + 1) * HEAD_DIM]
        partner = jnp.where(first_half, pltpu.roll(th, HEAD_DIM - ROPE_PAIRS, 1),
                            pltpu.roll(th, ROPE_PAIRS, 1))
        outs.append(th * cos + partner * sin)
    return jnp.concatenate(outs, axis=1)


def _inproj_kernel(*refs, tm, host):
    x_ref, m_ref, g_ref, w_ref, cos_ref, sin_ref = refs[:6]
    o_ref, xn_ref = refs[6 + host], refs[-1]
    j = pl.program_id(1)
    q_blocks = K_OFF // PROJ_TN
    plain_end = GATE_OFF // PROJ_TN

    def project(epilogue):
        for src, dst in zip(refs[6:6 + host], refs[7 + host:-1]):
            dst[...] = src[...].astype(BF16)
        for c0 in range(0, PROJ_TN, PROJ_CHUNK):
            cs = slice(c0, c0 + PROJ_CHUNK)
            o_ref[:, cs] = epilogue(_dot(xn_ref[...], w_ref[:, cs]), c0).astype(BF16)

    def rope(acc):
        return _rope(acc, cos_ref[...], sin_ref[...])

    @pl.when(j == 0)
    def _():
        _norm_mod_rows(x_ref, xn_ref, g_ref, m_ref, 3, pl.program_id(0) * tm, tm)

    @pl.when(j < q_blocks)
    def _():
        project(lambda acc, c0: rope(acc) * (HEAD_DIM ** -0.5 * LOG2E))

    @pl.when(j == q_blocks)
    def _():
        project(lambda acc, c0: rope(acc) if c0 < V_OFF - K_OFF else acc)

    @pl.when((j > q_blocks) & (j < plain_end))
    def _():
        project(lambda acc, c0: acc)

    @pl.when(j >= plain_end)
    def _():
        project(lambda acc, c0: 0.5 * jnp.tanh(0.5 * acc) + 0.5)


def _inproj(x, mods, norm_g, w_bf, layer, cos, sin, ffn_wi, ffn_wo):
    tm = PROJ_TM
    n_i, n_j = ROWS // tm, IN_COLS // PROJ_TN
    row = lambda width: pl.BlockSpec((tm, width), lambda i, j: (i, 0))
    step = lambda i, j: i * n_j + j
    hosted = [_hosted_cast(ffn_wi, layer, 1, HOST_TN, step, n_i * n_j),
              _hosted_cast(ffn_wo, layer, 0, HOST_TN // 2, step, n_i * n_j)]
    outs = pl.pallas_call(
        functools.partial(_inproj_kernel, tm=tm, host=len(hosted)),
        grid=(n_i, n_j),
        in_specs=[row(D_MODEL)] + _layer_vector_specs(layer, 1) + [
            pl.BlockSpec((D_MODEL, PROJ_TN), lambda i, j: (0, j)),
            row(HEAD_DIM), row(HEAD_DIM)] + [h[0] for h in hosted],
        out_specs=[pl.BlockSpec((tm, PROJ_TN), lambda i, j: (i, j))] + [h[1] for h in hosted],
        out_shape=[jax.ShapeDtypeStruct((ROWS, IN_COLS), BF16)] + [h[2] for h in hosted],
        scratch_shapes=[pltpu.VMEM((tm, D_MODEL), BF16)],
        compiler_params=_cparams(("arbitrary", "arbitrary")),
        name="inproj",
    )(x, mods, norm_g.reshape(DEPTH * 3, 1, D_MODEL), w_bf, cos, sin, ffn_wi, ffn_wo)
    return outs[0], (outs[1], outs[2])


def _rope_tables():
    pos = np.arange(SEQ)
    inv = ROPE_BASE ** (-np.arange(ROPE_PAIRS, dtype=np.float64) / ROPE_PAIRS)
    ang_r = (pos // GRID_W)[:, None] * inv
    ang_c = (pos % GRID_W)[:, None] * inv
    cos = np.concatenate([np.cos(ang_r), np.cos(ang_r), np.cos(ang_c), np.cos(ang_c)], axis=1)
    sin = np.concatenate([-np.sin(ang_r), np.sin(ang_r), -np.sin(ang_c), np.sin(ang_c)], axis=1)
    cos = np.concatenate([cos, np.ones((CTX_LEN, HEAD_DIM))], axis=0)
    sin = np.concatenate([sin, np.zeros((CTX_LEN, HEAD_DIM))], axis=0)
    return jnp.asarray(cos, F32), jnp.asarray(sin, F32)


def _attn_kernel(sink_ref, q_ref, kp_ref, kc_ref, kn_ref, vp_ref, vc_ref, vn_ref, kx_ref, vx_ref,
                 *rest, layer):
    n_w = len(MIX_WEIGHT_ROWS)
    w_in, o_ref, w_out = rest[:n_w], rest[n_w], rest[n_w + 1:]
    for src, dst in zip(w_in, w_out):
        dst[...] = src[...].astype(BF16)
    n = pl.program_id(0)
    n_lat = SEQ // BLOCK
    r = lax.broadcasted_iota(jnp.int32, (BLOCK, BLOCK), 0)
    c = lax.broadcasted_iota(jnp.int32, (BLOCK, BLOCK), 1)
    is_lat = n < n_lat
    prev_ok = (c >= r) & (n >= 1) & is_lat
    next_ok = (c <= r) & (n < n_lat - 1)
    n_keys = 3 * BLOCK + CTX_LEN
    ones = jnp.ones((n_keys, HEAD_DIM), BF16)
    contract_last = (((1,), (1,)), ((), ()))
    for h in range(N_KV_HEADS):
        hs = slice(h * HEAD_DIM, (h + 1) * HEAD_DIM)
        qg = jnp.concatenate(
            [q_ref[:, (h * GQA_GROUP + g) * HEAD_DIM:(h * GQA_GROUP + g + 1) * HEAD_DIM]
             for g in range(GQA_GROUP)], axis=0)
        kb = jnp.concatenate([kp_ref[:, hs], kc_ref[:, hs], kn_ref[:, hs], kx_ref[:, hs]], axis=0)
        vb = jnp.concatenate([vp_ref[:, hs], vc_ref[:, hs], vn_ref[:, hs], vx_ref[:, hs]], axis=0)
        vb = jnp.concatenate([vb, ones], axis=1)
        s = lax.dot_general(qg, kb, contract_last, preferred_element_type=F32)
        probs, maxes, sinks = [], [], []
        for g in range(GQA_GROUP):
            sg = s[g * BLOCK:(g + 1) * BLOCK]
            parts = [jnp.where(prev_ok, sg[:, 0:BLOCK], NEG_INF),
                     jnp.where(is_lat, sg[:, BLOCK:2 * BLOCK], NEG_INF),
                     jnp.where(next_ok, sg[:, 2 * BLOCK:3 * BLOCK], NEG_INF),
                     sg[:, 3 * BLOCK:4 * BLOCK], sg[:, 4 * BLOCK:5 * BLOCK]]
            sink = sink_ref[layer * N_Q_HEADS + h * GQA_GROUP + g] * LOG2E
            lane_max = jnp.maximum(jnp.maximum(jnp.maximum(parts[0], parts[1]),
                                               jnp.maximum(parts[2], parts[3])), parts[4])
            m = jnp.maximum(jnp.max(lane_max, axis=-1, keepdims=True), sink)
            probs.append(jnp.concatenate([jnp.exp2(pt - m).astype(BF16) for pt in parts], axis=1))
            maxes.append(m)
            sinks.append(sink)
        oe = _dot(jnp.concatenate(probs, axis=0), vb)
        for g in range(GQA_GROUP):
            hq = h * GQA_GROUP + g
            og = oe[g * BLOCK:(g + 1) * BLOCK]
            denom = og[:, HEAD_DIM:HEAD_DIM + 1] + jnp.exp2(sinks[g] - maxes[g])
            o_ref[:, hq * HEAD_DIM:(hq + 1) * HEAD_DIM] = (og[:, :HEAD_DIM] / denom).astype(BF16)


def _attention(p, sinks, layer, n_rows, mix_weights):
    n_blocks = n_rows // BLOCK
    last = ROWS // BLOCK - 1
    kcol, vcol = K_OFF // 512, V_OFF // 512
    ctx_blk = SEQ // CTX_LEN
    kv = lambda col, off: pl.BlockSpec(
        (BLOCK, 512), lambda n, *_: (jnp.clip(n + off, 0, last), col))
    n_slabs = SEQ // BLOCK
    slab = lambda n: jnp.minimum(n, n_slabs - 1)
    w_in_specs, w_out_specs, w_out_shapes = [], [], []
    for w, rows in zip(mix_weights, MIX_WEIGHT_ROWS):
        assert w.shape[1] == rows
        sr, width = rows // n_slabs, w.shape[2]
        w_in_specs.append(pl.BlockSpec((None, sr, width), lambda n, *_: (layer, slab(n), 0)))
        w_out_specs.append(pl.BlockSpec((sr, width), lambda n, *_: (slab(n), 0)))
        w_out_shapes.append(jax.ShapeDtypeStruct((rows, width), BF16))
    outs = pl.pallas_call(
        functools.partial(_attn_kernel, layer=layer),
        grid_spec=pltpu.PrefetchScalarGridSpec(
            num_scalar_prefetch=1,
            grid=(n_blocks,),
            in_specs=[
                pl.BlockSpec((BLOCK, D_MODEL), lambda n, *_: (n, 0)),
                kv(kcol, -1), kv(kcol, 0), kv(kcol, 1),
                kv(vcol, -1), kv(vcol, 0), kv(vcol, 1),
                pl.BlockSpec((CTX_LEN, 512), lambda n, *_: (ctx_blk, kcol)),
                pl.BlockSpec((CTX_LEN, 512), lambda n, *_: (ctx_blk, vcol)),
            ] + w_in_specs,
            out_specs=[pl.BlockSpec((BLOCK, D_MODEL), lambda n, *_: (n, 0))] + w_out_specs,
        ),
        out_shape=[jax.ShapeDtypeStruct((n_rows, D_MODEL), BF16)] + w_out_shapes,
        compiler_params=_cparams(("arbitrary",)),
        name="attention",
    )(sinks, p, p, p, p, p, p, p, p, p, *mix_weights)
    return outs[0], outs[1:]


def _mix_kernel(x_ref, a_ref, u_ref, up_ref, un_ref, cb_ref, cc_ref, ccp_ref, ccn_ref,
                cx_ref, cxp_ref, cxn_ref, g0a_ref, g0b_ref, g1a_ref, g1b_ref, g2a_ref, g2b_ref,
                m_ref, pw_ref, ps_ref, cw_ref, wa_ref, wp_ref, wc_ref, wo_ref, o_ref, ub_ref):
    t = pl.program_id(0)
    tm = MIX_TM
    lat_tiles = SEQ // tm
    is_ctx = t >= lat_tiles
    seg_first = (t == 0) | is_ctx
    seg_last = (t == lat_tiles - 1) | is_ctx
    seg_len = jnp.where(is_ctx, CTX_LEN, SEQ)
    row = lax.broadcasted_iota(jnp.int32, (tm, 1), 0)
    pos = jnp.where(is_ctx, 0, t * tm) + row

    y_attn = _dot(a_ref[...], wa_ref[...])

    ub_ref[0:tm, :] = u_ref[...]
    ub_ref[tm:tm + HALO, :] = jnp.where(seg_last, jnp.zeros_like(un_ref[...]), un_ref[...])
    ub_ref[tm + HALO:POOL_K - HALO, :] = jnp.zeros((POOL_K - 2 * HALO - tm, POOL_WIDTH), BF16)
    ub_ref[POOL_K - HALO:POOL_K, :] = jnp.where(seg_first, jnp.zeros_like(up_ref[...]), up_ref[...])
    br = lax.broadcasted_iota(jnp.int32, (tm, POOL_K), 0)
    bc = lax.broadcasted_iota(jnp.int32, (tm, POOL_K), 1)
    d = bc - br
    d = jnp.where(d >= POOL_K // 2, d - POOL_K, d)
    pooled = []
    for gi, w in enumerate(POOL_WINDOWS):
        lo, hi = w // 2, w - w // 2 - 1
        gs = slice(gi * POOL_GROUP, (gi + 1) * POOL_GROUP)
        band = ((d >= -lo) & (d <= hi)).astype(BF16)
        wsum = _dot(band, ub_ref[:, gs])
        cnt = (jnp.minimum(pos + hi, seg_len - 1) - jnp.maximum(pos - lo, 0) + 1).astype(F32)
        y = wsum / cnt - u_ref[:, gs].astype(F32)
        pooled.append(_dot(y.astype(BF16), pw_ref[gs, :]))
    y_pool = jnp.concatenate(pooled, axis=1) * ps_ref[...]
    y_pool = _dot(y_pool.astype(BF16), wp_ref[...])

    z = cc_ref[...].astype(F32) * cx_ref[...].astype(F32)
    z_prev = ccp_ref[HALO - 1:HALO, :].astype(F32) * cxp_ref[HALO - 1:HALO, :].astype(F32)
    z_next = ccn_ref[0:1, :].astype(F32) * cxn_ref[0:1, :].astype(F32)
    z_prev = jnp.where(seg_first, 0.0, z_prev)
    z_next = jnp.where(seg_last, 0.0, z_next)
    zm1 = jnp.where(row == 0, z_prev, pltpu.roll(z, 1, 0))
    zp1 = jnp.where(row == tm - 1, z_next, pltpu.roll(z, tm - 1, 0))
    conv = zm1 * cw_ref[0:1, :] + z * cw_ref[1:2, :] + zp1 * cw_ref[2:3, :]
    y_conv = _dot((cb_ref[...].astype(F32) * conv).astype(BF16), wc_ref[...])

    halves = []
    for k, (g0, g1, g2) in enumerate(((g0a_ref, g1a_ref, g2a_ref), (g0b_ref, g1b_ref, g2b_ref))):
        cs = slice(k * GATE_TN, (k + 1) * GATE_TN)
        halves.append((g0[...].astype(F32) * y_attn[:, cs] + g1[...].astype(F32) * y_pool[:, cs]
                       + g2[...].astype(F32) * y_conv[:, cs]).astype(BF16))
    out = _dot(jnp.concatenate(halves, axis=1), wo_ref[...])
    o_ref[...] = x_ref[...] + _mod_rows(m_ref, 5, is_ctx) * out


def _mix(x, p, attn, mods, layer, pool_scale, conv_w, weights_bf, n_rows):
    w_attn_out, w_pool_out, w_conv_out, w_o, pool_w = weights_bf
    tm = MIX_TM
    n_tiles = n_rows // tm
    hb = tm // HALO
    last_halo = ROWS // HALO - 1
    tile = lambda width, col: pl.BlockSpec((tm, width), lambda t: (t, col))
    prev = lambda col: pl.BlockSpec((HALO, POOL_WIDTH), lambda t: (jnp.maximum(t * hb - 1, 0), col))
    nxt = lambda col: pl.BlockSpec((HALO, POOL_WIDTH), lambda t: (jnp.minimum((t + 1) * hb, last_halo), col))
    const = lambda shape: pl.BlockSpec(shape, lambda t: (0,) * len(shape), pipeline_mode=pl.Buffered(1))
    ucol, cbcol, cccol, cxcol = POOL_OFF // 1024, CB_OFF // 1024, CC_OFF // 1024, CX_OFF // 1024
    gcol = GATE_OFF // GATE_TN
    return pl.pallas_call(
        _mix_kernel,
        grid=(n_tiles,),
        in_specs=[
            tile(D_MODEL, 0),
            tile(D_MODEL, 0),
            tile(1024, ucol), prev(ucol), nxt(ucol),
            tile(1024, cbcol),
            tile(1024, cccol), prev(cccol), nxt(cccol),
            tile(1024, cxcol), prev(cxcol), nxt(cxcol),
            *[tile(GATE_TN, gcol + k) for k in range(3 * D_MODEL // GATE_TN)],
            pl.BlockSpec((None, 8, N_MODS * D_MODEL), lambda t: (layer, 0, 0), pipeline_mode=pl.Buffered(1)),
            const((POOL_WIDTH, POOL_GROUP)),
            pl.BlockSpec((None, 1, POOL_WIDTH), lambda t: (layer, 0, 0), pipeline_mode=pl.Buffered(1)),
            pl.BlockSpec((None, 3, CONV_WIDTH), lambda t: (layer, 0, 0), pipeline_mode=pl.Buffered(1)),
            const((D_MODEL, D_MODEL)),
            const((POOL_WIDTH, D_MODEL)),
            const((CONV_WIDTH, D_MODEL)),
            const((D_MODEL, D_MODEL)),
        ],
        out_specs=pl.BlockSpec((tm, D_MODEL), lambda t: (t, 0)),
        out_shape=jax.ShapeDtypeStruct((n_rows, D_MODEL), F32),
        scratch_shapes=[pltpu.VMEM((POOL_K, POOL_WIDTH), BF16)],
        compiler_params=_cparams(("arbitrary",)),
        name="mix",
    )(x, attn, *([p] * 16), mods, pool_w, pool_scale.reshape(DEPTH, 1, POOL_WIDTH),
      conv_w, w_attn_out, w_pool_out, w_conv_out, w_o)


def kernel(x, c, ctx, c_ctx, w_ada, b_ada, norm_g, ffn1_wi, ffn1_wo, w_in, attn_sink, pool_w, pool_scale,
           conv_w, w_attn_out, w_pool_out, w_conv_out, w_o, ffn2_wi, ffn2_wo, final_g):
    assert x.shape == (1, SEQ, D_MODEL) and ctx.shape == (1, CTX_LEN, D_MODEL)
    cond = jnp.concatenate([c.reshape(1, D_MODEL), c_ctx.reshape(1, D_MODEL),
                            jnp.zeros((6, D_MODEL), F32)], axis=0)
    mods = _ada(cond, w_ada, b_ada)
    cos, sin = _rope_tables()
    mix_weights = (w_attn_out, w_pool_out, w_conv_out, w_o,
                   pool_w.reshape(DEPTH, POOL_WIDTH, POOL_GROUP))
    sinks = attn_sink.reshape(DEPTH * N_Q_HEADS)

    h = jnp.concatenate([x[0], ctx[0]], axis=0)
    ffn1_bf = None
    for l in range(DEPTH):
        last = l == DEPTH - 1
        h, (w_in_bf,) = _ffn(h, FFN_TM, FFN_TF, mods, norm_g, ffn1_wi, ffn1_wo, l, 0, final_g,
                             hosted=[(w_in, l, 1, HOST_TN)], weights_bf=ffn1_bf)
        p, ffn2_bf = _inproj(h, mods, norm_g, w_in_bf, l, cos, sin, ffn2_wi, ffn2_wo)
        n_rows = SEQ if last else ROWS
        attn, mix_bf = _attention(p, sinks, l, n_rows, mix_weights)
        h = _mix(h, p, attn, mods, l, pool_scale, conv_w, mix_bf, n_rows)
        next_ffn1 = [] if last else [(ffn1_wi, l + 1, 1, HOST_TN // 2), (ffn1_wo, l + 1, 0, HOST_TN // 4)]
        h, ffn1_bf = _ffn(h, FFN_TM_LATENT if last else FFN_TM, FFN_TF, mods, norm_g, ffn2_wi, ffn2_wo, l, 6,
                          final_g, final=last, hosted=next_ffn1, weights_bf=ffn2_bf)
    return h.reshape(1, SEQ, D_MODEL)
```

```python
import functools

import jax
import jax.numpy as jnp
import numpy as np
from jax import lax
from jax.experimental import pallas as pl
from jax.experimental.pallas import tpu as pltpu

D_MODEL = 2048
SEQ = 8192
CTX_LEN = 256
ROWS = SEQ + CTX_LEN
DEPTH = 2
GRID_W = 64
HEAD_DIM = 128
N_Q_HEADS = 16
N_KV_HEADS = 4
GQA_GROUP = 4
WINDOW = 128
BLOCK = 128
ROPE_PAIRS = 32
ROPE_BASE = 10000.0
POOL_WINDOWS = (2, 4, 8, 16)
POOL_WIDTH = 1024
POOL_GROUP = 256
CONV_WIDTH = 1024
D_FF = 5632
N_MODS = 9
EPS = 1e-6
NEG_INF = -1e30
LOG2E = 1.4426950408889634

K_OFF = 2048
V_OFF = 2560
POOL_OFF = 3072
CB_OFF = 4096
CC_OFF = 5120
CX_OFF = 6144
GATE_OFF = 7168
IN_COLS = GATE_OFF + 3 * D_MODEL

BF16 = jnp.bfloat16
F32 = jnp.float32

VMEM_LIMIT_BYTES = 62 * 1024 * 1024

FFN_TM = 1056
FFN_TM_LATENT = 1024
PROJ_TM = 1408
FFN_TF = 512
HOST_ROWS = 32
HOST_ROWS_WO = 64
FFN_TF_CAST = 256
PROJ_TN = 1024
PROJ_CHUNK = 512
ROW_CHUNK = 32
ROWS_UNROLL = 4
HIDDEN_CHUNK = 256
COL_CHUNK = 512
MIX_TM = 256
MIX_WEIGHT_ROWS = (2048, 1024, 1024, 2048, 1024)
GATE_TN = 1024
HALO = 16
POOL_K = 384


def _cparams(sem):
    return pltpu.CompilerParams(dimension_semantics=sem, vmem_limit_bytes=VMEM_LIMIT_BYTES)


def _layer_vector_specs(layer, norm_idx):
    once = pl.Buffered(1)
    return [pl.BlockSpec((None, 8, N_MODS * D_MODEL), lambda *_: (layer, 0, 0), pipeline_mode=once),
            pl.BlockSpec((None, 1, D_MODEL), lambda *_: (layer * 3 + norm_idx, 0, 0), pipeline_mode=once)]


def _hosted_cast(w, layer, axis, size, step, n_steps):
    rows, cols = w.shape[1:]
    n_slabs = (rows, cols)[axis] // size
    assert n_slabs * size == (rows, cols)[axis] and n_slabs <= n_steps
    block = (size, cols) if axis == 0 else (rows, size)

    def index(*ids):
        k = jnp.minimum(step(*ids), n_slabs - 1)
        return (k, 0) if axis == 0 else (0, k)

    return (pl.BlockSpec((None,) + block, lambda *ids: (layer,) + index(*ids)),
            pl.BlockSpec(block, index),
            jax.ShapeDtypeStruct((rows, cols), BF16))


def _dot(a, b):
    return jnp.dot(a, b, preferred_element_type=F32)


def _norm_mod(x, g, shift, scale):
    y = x * lax.rsqrt(jnp.mean(x * x, axis=-1, keepdims=True) + EPS)
    return y * (g * (1.0 + scale)) + shift


def _mod_rows(m_ref, k, is_ctx):
    lo, hi = k * D_MODEL, (k + 1) * D_MODEL
    return jnp.where(is_ctx, m_ref[1:2, lo:hi], m_ref[0:1, lo:hi])


assert SEQ % ROW_CHUNK == 0


def _is_ctx_rows(first_row):
    return first_row >= SEQ


def _rows_loop(tm, body, unroll):
    def step(r, carry):
        body(pl.multiple_of(r * ROW_CHUNK, ROW_CHUNK))
        return carry
    lax.fori_loop(0, tm // ROW_CHUNK, step, 0, unroll=unroll)


def _norm_mod_rows(x_ref, xm_ref, g_ref, m_ref, k_shift, row0, tm):
    def rows(r0):
        rs = pl.ds(r0, ROW_CHUNK)
        is_ctx = _is_ctx_rows(row0 + r0)
        xm_ref[rs, :] = _norm_mod(x_ref[rs, :], g_ref[...], _mod_rows(m_ref, k_shift, is_ctx),
                                  _mod_rows(m_ref, k_shift + 1, is_ctx)).astype(BF16)
    _rows_loop(tm, rows, ROWS_UNROLL)


def _ada_kernel(c_ref, w_ref, b_ref, o_ref):
    c = c_ref[...]
    s = c * jax.nn.sigmoid(c)
    o_ref[...] = _dot(s.astype(BF16), w_ref[...].astype(BF16)) + b_ref[...]


def _ada(cond, w_ada, b_ada):
    tn = 1024
    n_cols = N_MODS * D_MODEL
    return pl.pallas_call(
        _ada_kernel,
        grid=(DEPTH, n_cols // tn),
        in_specs=[
            pl.BlockSpec((8, D_MODEL), lambda l, j: (0, 0)),
            pl.BlockSpec((None, D_MODEL, tn), lambda l, j: (l, 0, j)),
            pl.BlockSpec((None, 1, tn), lambda l, j: (l, 0, j)),
        ],
        out_specs=pl.BlockSpec((None, 8, tn), lambda l, j: (l, 0, j)),
        out_shape=jax.ShapeDtypeStruct((DEPTH, 8, n_cols), F32),
        compiler_params=_cparams(("arbitrary", "arbitrary")),
        name="ada",
    )(cond, w_ada, b_ada.reshape(DEPTH, 1, n_cols))


def _ffn_kernel(*refs, tm, tile0, k0, final, cast, host):
    if cast:
        (x_ref, m_ref, g_ref, wig_in, wiu_in, wo_in, fg_ref,
         o_ref, wig_ref, wiu_ref, wo_ref, xm_ref) = refs
    else:
        x_ref, m_ref, g_ref, wig_ref, wiu_ref, wo_ref, fg_ref = refs[:7]
        o_ref, xm_ref = refs[7 + host], refs[-1]
    row0 = (pl.program_id(0) + tile0) * tm
    f = pl.program_id(1)
    last_f = pl.num_programs(1) - 1

    @pl.when(f == 0)
    def _():
        _norm_mod_rows(x_ref, xm_ref, g_ref, m_ref, k0, row0, tm)
        o_ref[...] = jnp.zeros_like(o_ref)

    def hidden_block(finish):
        if cast:
            wig_ref[...] = wig_in[...].astype(BF16)
            wiu_ref[...] = wiu_in[...].astype(BF16)
            wo_ref[...] = wo_in[...].astype(BF16)
        else:
            for src, dst in zip(refs[7:7 + host], refs[8 + host:-1]):
                dst[...] = src[...].astype(BF16)
        xm = xm_ref[...]
        parts = []
        for h0 in range(0, wig_ref.shape[1], HIDDEN_CHUNK):
            hs = slice(h0, h0 + HIDDEN_CHUNK)
            g = _dot(xm, wig_ref[:, hs])
            u = _dot(xm, wiu_ref[:, hs])
            parts.append(((g * jax.nn.sigmoid(g)) * u).astype(BF16))
        a = parts[0] if len(parts) == 1 else jnp.concatenate(parts, axis=1)
        if finish:
            is_ctx = (row0 + lax.broadcasted_iota(jnp.int32, (tm, 1), 0)) >= SEQ
            lo = (k0 + 2) * D_MODEL
        for c0 in range(0, D_MODEL, COL_CHUNK):
            cs = slice(c0, c0 + COL_CHUNK)
            acc = o_ref[:, cs] + _dot(a, wo_ref[:, cs])
            if finish:
                gate = jnp.where(is_ctx, m_ref[1:2, lo + c0:lo + c0 + COL_CHUNK],
                                 m_ref[0:1, lo + c0:lo + c0 + COL_CHUNK])
                acc = x_ref[:, cs] + (0.5 * gate) * acc
            o_ref[:, cs] = acc

    if final:
        hidden_block(False)

        @pl.when(f == last_f)
        def _():
            for r0 in range(0, tm, ROW_CHUNK):
                rs = pl.ds(r0, ROW_CHUNK)
                gate = _mod_rows(m_ref, k0 + 2, _is_ctx_rows(row0 + r0))
                y = x_ref[rs, :] + (0.5 * gate) * o_ref[rs, :]
                o_ref[rs, :] = (y * lax.rsqrt(jnp.mean(y * y, axis=-1, keepdims=True) + EPS)) * fg_ref[...]
    else:
        pl.when(f < last_f)(functools.partial(hidden_block, False))
        pl.when(f == last_f)(functools.partial(hidden_block, True))


def _ffn(x, tm, tf, mods, norm_g, wi, wo, layer, k0, final_g, final=False, hosted=(), weights_bf=None):
    n_rows = x.shape[0]
    kern = functools.partial(_ffn_kernel, tm=tm, k0=k0, final=final)
    vec = lambda i, f: (0, 0)
    small = _layer_vector_specs(layer, k0 // 3)
    scratch = [pltpu.VMEM((tm, D_MODEL), BF16)]
    g2, fg2 = norm_g.reshape(DEPTH * 3, 1, D_MODEL), final_g.reshape(1, D_MODEL)
    n_f = D_FF // tf
    if weights_bf is not None:
        wig = wiu = weights_bf[0]
        wo_bf, up_block0, tile0 = weights_bf[1], n_f, 0
    else:
        x, wig, wiu, wo_bf = _ffn_cast_tile(x, kern, small, scratch, mods, g2, fg2, wi, wo, layer, tm, final)
        up_block0, tile0 = 0, 1

    n_i = n_rows // tm - tile0
    in_specs = [pl.BlockSpec((tm, D_MODEL), lambda i, f: (i + tile0, 0))] + small + [
        pl.BlockSpec((D_MODEL, tf), lambda i, f: (0, f)),
        pl.BlockSpec((D_MODEL, tf), lambda i, f: (0, up_block0 + f)),
        pl.BlockSpec((tf, D_MODEL), lambda i, f: (f, 0)),
        pl.BlockSpec((1, D_MODEL), vec)]
    out_specs = [pl.BlockSpec((tm, D_MODEL), lambda i, f: (i + tile0, 0))]
    out_shape = [jax.ShapeDtypeStruct((n_rows, D_MODEL), F32)]
    operands = [x, mods, g2, wig, wiu, wo_bf, fg2]
    for w, w_layer, axis, size in hosted:
        spec_in, spec_out, shape = _hosted_cast(w, w_layer, axis, size, lambda i, f: i * n_f + f, n_i * n_f)
        in_specs.append(spec_in)
        out_specs.append(spec_out)
        out_shape.append(shape)
        operands.append(w)
    outs = pl.pallas_call(
        functools.partial(kern, tile0=tile0, cast=False, host=len(hosted)),
        grid=(n_i, n_f),
        in_specs=in_specs,
        out_specs=out_specs,
        out_shape=out_shape,
        input_output_aliases={0: 0},
        scratch_shapes=scratch,
        compiler_params=_cparams(("arbitrary", "arbitrary")),
        name="ffn_final" if final else "ffn",
    )(*operands)
    return outs[0], tuple(outs[1:])


def _ffn_cast_tile(x, kern, small, scratch, mods, g2, fg2, wi, wo, layer, tm, final):
    n_rows = x.shape[0]
    vec = lambda i, f: (0, 0)
    tc = FFN_TF_CAST
    n_fc = D_FF // tc
    return pl.pallas_call(
        functools.partial(kern, tile0=0, cast=True, host=0),
        grid=(1, n_fc),
        in_specs=[pl.BlockSpec((tm, D_MODEL), lambda i, f: (0, 0), pipeline_mode=pl.Buffered(1))] + small + [
            pl.BlockSpec((None, D_MODEL, tc), lambda i, f: (layer, 0, f)),
            pl.BlockSpec((None, D_MODEL, tc), lambda i, f: (layer, 0, n_fc + f)),
            pl.BlockSpec((None, tc, D_MODEL), lambda i, f: (layer, f, 0)),
            pl.BlockSpec((1, D_MODEL), vec)],
        out_specs=[pl.BlockSpec((tm, D_MODEL), lambda i, f: (0, 0)),
                   pl.BlockSpec((D_MODEL, tc), lambda i, f: (0, f)),
                   pl.BlockSpec((D_MODEL, tc), lambda i, f: (0, f)),
                   pl.BlockSpec((tc, D_MODEL), lambda i, f: (f, 0))],
        out_shape=[jax.ShapeDtypeStruct((n_rows, D_MODEL), F32),
                   jax.ShapeDtypeStruct((D_MODEL, D_FF), BF16),
                   jax.ShapeDtypeStruct((D_MODEL, D_FF), BF16),
                   jax.ShapeDtypeStruct((D_FF, D_MODEL), BF16)],
        input_output_aliases={0: 0},
        scratch_shapes=scratch,
        compiler_params=_cparams(("arbitrary", "arbitrary")),
        name="ffn_cast_final" if final else "ffn_cast",
    )(x, mods, g2, wi, wi, wo, fg2)


def _rope(t, cos, sin):
    lane = lax.broadcasted_iota(jnp.int32, (1, HEAD_DIM), 1)
    first_half = (lane % (2 * ROPE_PAIRS)) < ROPE_PAIRS
    outs = []
    for hd in range(t.shape[1] // HEAD_DIM):
        th = t[:, hd * HEAD_DIM:(hd + 1) * HEAD_DIM]
        partner = jnp.where(first_half, pltpu.roll(th, HEAD_DIM - ROPE_PAIRS, 1),
                            pltpu.roll(th, ROPE_PAIRS, 1))
        outs.append(th * cos + partner * sin)
    return jnp.concatenate(outs, axis=1)


def _inproj_kernel(*refs, tm, host):
    x_ref, m_ref, g_ref, w_ref, cos_ref, sin_ref = refs[:6]
    o_ref, xn_ref = refs[6 + host], refs[-1]
    j = pl.program_id(1)
    q_blocks = K_OFF // PROJ_TN
    plain_end = GATE_OFF // PROJ_TN

    def project(epilogue):
        for src, dst in zip(refs[6:6 + host], refs[7 + host:-1]):
            dst[...] = src[...].astype(BF16)
        for c0 in range(0, PROJ_TN, PROJ_CHUNK):
            cs = slice(c0, c0 + PROJ_CHUNK)
            o_ref[:, cs] = epilogue(_dot(xn_ref[...], w_ref[:, cs]), c0).astype(BF16)

    def rope(acc):
        return _rope(acc, cos_ref[...], sin_ref[...])

    @pl.when(j == 0)
    def _():
        _norm_mod_rows(x_ref, xn_ref, g_ref, m_ref, 3, pl.program_id(0) * tm, tm)

    @pl.when(j < q_blocks)
    def _():
        project(lambda acc, c0: rope(acc) * (HEAD_DIM ** -0.5 * LOG2E))

    @pl.when(j == q_blocks)
    def _():
        project(lambda acc, c0: rope(acc) if c0 < V_OFF - K_OFF else acc)

    @pl.when((j > q_blocks) & (j < plain_end))
    def _():
        project(lambda acc, c0: acc)

    @pl.when(j >= plain_end)
    def _():
        project(lambda acc, c0: 0.5 * jnp.tanh(0.5 * acc) + 0.5)


def _inproj(x, mods, norm_g, w_bf, layer, cos, sin, ffn_wi, ffn_wo):
    tm = PROJ_TM
    n_i, n_j = ROWS // tm, IN_COLS // PROJ_TN
    row = lambda width: pl.BlockSpec((tm, width), lambda i, j: (i, 0))
    step = lambda i, j: i * n_j + j
    hosted = [_hosted_cast(ffn_wi, layer, 0, HOST_ROWS, step, n_i * n_j),
              _hosted_cast(ffn_wo, layer, 0, 2 * HOST_ROWS_WO, step, n_i * n_j)]
    outs = pl.pallas_call(
        functools.partial(_inproj_kernel, tm=tm, host=len(hosted)),
        grid=(n_i, n_j),
        in_specs=[row(D_MODEL)] + _layer_vector_specs(layer, 1) + [
            pl.BlockSpec((D_MODEL, PROJ_TN), lambda i, j: (0, j)),
            row(HEAD_DIM), row(HEAD_DIM)] + [h[0] for h in hosted],
        out_specs=[pl.BlockSpec((tm, PROJ_TN), lambda i, j: (i, j))] + [h[1] for h in hosted],
        out_shape=[jax.ShapeDtypeStruct((ROWS, IN_COLS), BF16)] + [h[2] for h in hosted],
        scratch_shapes=[pltpu.VMEM((tm, D_MODEL), BF16)],
        compiler_params=_cparams(("arbitrary", "arbitrary")),
        name="inproj",
    )(x, mods, norm_g.reshape(DEPTH * 3, 1, D_MODEL), w_bf, cos, sin, ffn_wi, ffn_wo)
    return outs[0], (outs[1], outs[2])


def _rope_tables():
    pos = np.arange(SEQ)
    inv = ROPE_BASE ** (-np.arange(ROPE_PAIRS, dtype=np.float64) / ROPE_PAIRS)
    ang_r = (pos // GRID_W)[:, None] * inv
    ang_c = (pos % GRID_W)[:, None] * inv
    cos = np.concatenate([np.cos(ang_r), np.cos(ang_r), np.cos(ang_c), np.cos(ang_c)], axis=1)
    sin = np.concatenate([-np.sin(ang_r), np.sin(ang_r), -np.sin(ang_c), np.sin(ang_c)], axis=1)
    cos = np.concatenate([cos, np.ones((CTX_LEN, HEAD_DIM))], axis=0)
    sin = np.concatenate([sin, np.zeros((CTX_LEN, HEAD_DIM))], axis=0)
    return jnp.asarray(cos, F32), jnp.asarray(sin, F32)


def _attn_kernel(sink_ref, q_ref, kp_ref, kc_ref, kn_ref, vp_ref, vc_ref, vn_ref, kx_ref, vx_ref,
                 *rest, layer):
    n_w = len(MIX_WEIGHT_ROWS)
    w_in, o_ref, w_out = rest[:n_w], rest[n_w], rest[n_w + 1:]
    for src, dst in zip(w_in, w_out):
        dst[...] = src[...].astype(BF16)
    n = pl.program_id(0)
    n_lat = SEQ // BLOCK
    r = lax.broadcasted_iota(jnp.int32, (BLOCK, BLOCK), 0)
    c = lax.broadcasted_iota(jnp.int32, (BLOCK, BLOCK), 1)
    is_lat = n < n_lat
    prev_ok = (c >= r) & (n >= 1) & is_lat
    next_ok = (c <= r) & (n < n_lat - 1)
    n_keys = 3 * BLOCK + CTX_LEN
    ones = jnp.ones((n_keys, HEAD_DIM), BF16)
    contract_last = (((1,), (1,)), ((), ()))
    for h in range(N_KV_HEADS):
        hs = slice(h * HEAD_DIM, (h + 1) * HEAD_DIM)
        qg = jnp.concatenate(
            [q_ref[:, (h * GQA_GROUP + g) * HEAD_DIM:(h * GQA_GROUP + g + 1) * HEAD_DIM]
             for g in range(GQA_GROUP)], axis=0)
        kb = jnp.concatenate([kp_ref[:, hs], kc_ref[:, hs], kn_ref[:, hs], kx_ref[:, hs]], axis=0)
        vb = jnp.concatenate([vp_ref[:, hs], vc_ref[:, hs], vn_ref[:, hs], vx_ref[:, hs]], axis=0)
        vb = jnp.concatenate([vb, ones], axis=1)
        s = lax.dot_general(qg, kb, contract_last, preferred_element_type=F32)
        probs, maxes, sinks = [], [], []
        for g in range(GQA_GROUP):
            sg = s[g * BLOCK:(g + 1) * BLOCK]
            parts = [jnp.where(prev_ok, sg[:, 0:BLOCK], NEG_INF),
                     jnp.where(is_lat, sg[:, BLOCK:2 * BLOCK], NEG_INF),
                     jnp.where(next_ok, sg[:, 2 * BLOCK:3 * BLOCK], NEG_INF),
                     sg[:, 3 * BLOCK:4 * BLOCK], sg[:, 4 * BLOCK:5 * BLOCK]]
            sink = sink_ref[layer * N_Q_HEADS + h * GQA_GROUP + g] * LOG2E
            lane_max = jnp.maximum(jnp.maximum(jnp.maximum(parts[0], parts[1]),
                                               jnp.maximum(parts[2], parts[3])), parts[4])
            m = jnp.maximum(jnp.max(lane_max, axis=-1, keepdims=True), sink)
            probs.append(jnp.concatenate([jnp.exp2(pt - m).astype(BF16) for pt in parts], axis=1))
            maxes.append(m)
            sinks.append(sink)
        oe = _dot(jnp.concatenate(probs, axis=0), vb)
        for g in range(GQA_GROUP):
            hq = h * GQA_GROUP + g
            og = oe[g * BLOCK:(g + 1) * BLOCK]
            denom = og[:, HEAD_DIM:HEAD_DIM + 1] + jnp.exp2(sinks[g] - maxes[g])
            o_ref[:, hq * HEAD_DIM:(hq + 1) * HEAD_DIM] = (og[:, :HEAD_DIM] / denom).astype(BF16)


def _attention(p, sinks, layer, n_rows, mix_weights):
    n_blocks = n_rows // BLOCK
    last = ROWS // BLOCK - 1
    kcol, vcol = K_OFF // 512, V_OFF // 512
    ctx_blk = SEQ // CTX_LEN
    kv = lambda col, off: pl.BlockSpec(
        (BLOCK, 512), lambda n, *_: (jnp.clip(n + off, 0, last), col))
    n_slabs = SEQ // BLOCK
    slab = lambda n: jnp.minimum(n, n_slabs - 1)
    w_in_specs, w_out_specs, w_out_shapes = [], [], []
    for w, rows in zip(mix_weights, MIX_WEIGHT_ROWS):
        assert w.shape[1] == rows
        sr, width = rows // n_slabs, w.shape[2]
        w_in_specs.append(pl.BlockSpec((None, sr, width), lambda n, *_: (layer, slab(n), 0)))
        w_out_specs.append(pl.BlockSpec((sr, width), lambda n, *_: (slab(n), 0)))
        w_out_shapes.append(jax.ShapeDtypeStruct((rows, width), BF16))
    outs = pl.pallas_call(
        functools.partial(_attn_kernel, layer=layer),
        grid_spec=pltpu.PrefetchScalarGridSpec(
            num_scalar_prefetch=1,
            grid=(n_blocks,),
            in_specs=[
                pl.BlockSpec((BLOCK, D_MODEL), lambda n, *_: (n, 0)),
                kv(kcol, -1), kv(kcol, 0), kv(kcol, 1),
                kv(vcol, -1), kv(vcol, 0), kv(vcol, 1),
                pl.BlockSpec((CTX_LEN, 512), lambda n, *_: (ctx_blk, kcol)),
                pl.BlockSpec((CTX_LEN, 512), lambda n, *_: (ctx_blk, vcol)),
            ] + w_in_specs,
            out_specs=[pl.BlockSpec((BLOCK, D_MODEL), lambda n, *_: (n, 0))] + w_out_specs,
        ),
        out_shape=[jax.ShapeDtypeStruct((n_rows, D_MODEL), BF16)] + w_out_shapes,
        compiler_params=_cparams(("arbitrary",)),
        name="attention",
    )(sinks, p, p, p, p, p, p, p, p, p, *mix_weights)
    return outs[0], outs[1:]


def _mix_kernel(x_ref, a_ref, u_ref, up_ref, un_ref, cb_ref, cc_ref, ccp_ref, ccn_ref,
                cx_ref, cxp_ref, cxn_ref, g0a_ref, g0b_ref, g1a_ref, g1b_ref, g2a_ref, g2b_ref,
                m_ref, pw_ref, ps_ref, cw_ref, wa_ref, wp_ref, wc_ref, wo_ref, o_ref, ub_ref):
    t = pl.program_id(0)
    tm = MIX_TM
    lat_tiles = SEQ // tm
    is_ctx = t >= lat_tiles
    seg_first = (t == 0) | is_ctx
    seg_last = (t == lat_tiles - 1) | is_ctx
    seg_len = jnp.where(is_ctx, CTX_LEN, SEQ)
    row = lax.broadcasted_iota(jnp.int32, (tm, 1), 0)
    pos = jnp.where(is_ctx, 0, t * tm) + row

    y_attn = _dot(a_ref[...], wa_ref[...])

    ub_ref[0:tm, :] = u_ref[...]
    ub_ref[tm:tm + HALO, :] = jnp.where(seg_last, jnp.zeros_like(un_ref[...]), un_ref[...])
    ub_ref[tm + HALO:POOL_K - HALO, :] = jnp.zeros((POOL_K - 2 * HALO - tm, POOL_WIDTH), BF16)
    ub_ref[POOL_K - HALO:POOL_K, :] = jnp.where(seg_first, jnp.zeros_like(up_ref[...]), up_ref[...])
    br = lax.broadcasted_iota(jnp.int32, (tm, POOL_K), 0)
    bc = lax.broadcasted_iota(jnp.int32, (tm, POOL_K), 1)
    d = bc - br
    d = jnp.where(d >= POOL_K // 2, d - POOL_K, d)
    pooled = []
    for gi, w in enumerate(POOL_WINDOWS):
        lo, hi = w // 2, w - w // 2 - 1
        gs = slice(gi * POOL_GROUP, (gi + 1) * POOL_GROUP)
        band = ((d >= -lo) & (d <= hi)).astype(BF16)
        wsum = _dot(band, ub_ref[:, gs])
        cnt = (jnp.minimum(pos + hi, seg_len - 1) - jnp.maximum(pos - lo, 0) + 1).astype(F32)
        y = wsum / cnt - u_ref[:, gs].astype(F32)
        pooled.append(_dot(y.astype(BF16), pw_ref[gs, :]))
    y_pool = jnp.concatenate(pooled, axis=1) * ps_ref[...]
    y_pool = _dot(y_pool.astype(BF16), wp_ref[...])

    z = cc_ref[...].astype(F32) * cx_ref[...].astype(F32)
    z_prev = ccp_ref[HALO - 1:HALO, :].astype(F32) * cxp_ref[HALO - 1:HALO, :].astype(F32)
    z_next = ccn_ref[0:1, :].astype(F32) * cxn_ref[0:1, :].astype(F32)
    z_prev = jnp.where(seg_first, 0.0, z_prev)
    z_next = jnp.where(seg_last, 0.0, z_next)
    zm1 = jnp.where(row == 0, z_prev, pltpu.roll(z, 1, 0))
    zp1 = jnp.where(row == tm - 1, z_next, pltpu.roll(z, tm - 1, 0))
    conv = zm1 * cw_ref[0:1, :] + z * cw_ref[1:2, :] + zp1 * cw_ref[2:3, :]
    y_conv = _dot((cb_ref[...].astype(F32) * conv).astype(BF16), wc_ref[...])

    halves = []
    for k, (g0, g1, g2) in enumerate(((g0a_ref, g1a_ref, g2a_ref), (g0b_ref, g1b_ref, g2b_ref))):
        cs = slice(k * GATE_TN, (k + 1) * GATE_TN)
        halves.append((g0[...].astype(F32) * y_attn[:, cs] + g1[...].astype(F32) * y_pool[:, cs]
                       + g2[...].astype(F32) * y_conv[:, cs]).astype(BF16))
    out = _dot(jnp.concatenate(halves, axis=1), wo_ref[...])
    o_ref[...] = x_ref[...] + _mod_rows(m_ref, 5, is_ctx) * out


def _mix(x, p, attn, mods, layer, pool_scale, conv_w, weights_bf, n_rows):
    w_attn_out, w_pool_out, w_conv_out, w_o, pool_w = weights_bf
    tm = MIX_TM
    n_tiles = n_rows // tm
    hb = tm // HALO
    last_halo = ROWS // HALO - 1
    tile = lambda width, col: pl.BlockSpec((tm, width), lambda t: (t, col))
    prev = lambda col: pl.BlockSpec((HALO, POOL_WIDTH), lambda t: (jnp.maximum(t * hb - 1, 0), col))
    nxt = lambda col: pl.BlockSpec((HALO, POOL_WIDTH), lambda t: (jnp.minimum((t + 1) * hb, last_halo), col))
    const = lambda shape: pl.BlockSpec(shape, lambda t: (0,) * len(shape), pipeline_mode=pl.Buffered(1))
    ucol, cbcol, cccol, cxcol = POOL_OFF // 1024, CB_OFF // 1024, CC_OFF // 1024, CX_OFF // 1024
    gcol = GATE_OFF // GATE_TN
    return pl.pallas_call(
        _mix_kernel,
        grid=(n_tiles,),
        in_specs=[
            tile(D_MODEL, 0),
            tile(D_MODEL, 0),
            tile(1024, ucol), prev(ucol), nxt(ucol),
            tile(1024, cbcol),
            tile(1024, cccol), prev(cccol), nxt(cccol),
            tile(1024, cxcol), prev(cxcol), nxt(cxcol),
            *[tile(GATE_TN, gcol + k) for k in range(3 * D_MODEL // GATE_TN)],
            pl.BlockSpec((None, 8, N_MODS * D_MODEL), lambda t: (layer, 0, 0), pipeline_mode=pl.Buffered(1)),
            const((POOL_WIDTH, POOL_GROUP)),
            pl.BlockSpec((None, 1, POOL_WIDTH), lambda t: (layer, 0, 0), pipeline_mode=pl.Buffered(1)),
            pl.BlockSpec((None, 3, CONV_WIDTH), lambda t: (layer, 0, 0), pipeline_mode=pl.Buffered(1)),
            const((D_MODEL, D_MODEL)),
            const((POOL_WIDTH, D_MODEL)),
            const((CONV_WIDTH, D_MODEL)),
            const((D_MODEL, D_MODEL)),
        ],
        out_specs=pl.BlockSpec((tm, D_MODEL), lambda t: (t, 0)),
        out_shape=jax.ShapeDtypeStruct((n_rows, D_MODEL), F32),
        scratch_shapes=[pltpu.VMEM((POOL_K, POOL_WIDTH), BF16)],
        compiler_params=_cparams(("arbitrary",)),
        name="mix",
    )(x, attn, *([p] * 16), mods, pool_w, pool_scale.reshape(DEPTH, 1, POOL_WIDTH),
      conv_w, w_attn_out, w_pool_out, w_conv_out, w_o)


def kernel(x, c, ctx, c_ctx, w_ada, b_ada, norm_g, ffn1_wi, ffn1_wo, w_in, attn_sink, pool_w, pool_scale,
           conv_w, w_attn_out, w_pool_out, w_conv_out, w_o, ffn2_wi, ffn2_wo, final_g):
    assert x.shape == (1, SEQ, D_MODEL) and ctx.shape == (1, CTX_LEN, D_MODEL)
    cond = jnp.concatenate([c.reshape(1, D_MODEL), c_ctx.reshape(1, D_MODEL),
                            jnp.zeros((6, D_MODEL), F32)], axis=0)
    mods = _ada(cond, w_ada, b_ada)
    cos, sin = _rope_tables()
    mix_weights = (w_attn_out, w_pool_out, w_conv_out, w_o,
                   pool_w.reshape(DEPTH, POOL_WIDTH, POOL_GROUP))
    sinks = attn_sink.reshape(DEPTH * N_Q_HEADS)

    h = jnp.concatenate([x[0], ctx[0]], axis=0)
    ffn1_bf = None
    for l in range(DEPTH):
        last = l == DEPTH - 1
        h, (w_in_bf,) = _ffn(h, FFN_TM, FFN_TF, mods, norm_g, ffn1_wi, ffn1_wo, l, 0, final_g,
                             hosted=[(w_in, l, 0, HOST_ROWS)], weights_bf=ffn1_bf)
        p, ffn2_bf = _inproj(h, mods, norm_g, w_in_bf, l, cos, sin, ffn2_wi, ffn2_wo)
        n_rows = SEQ if last else ROWS
        attn, mix_bf = _attention(p, sinks, l, n_rows, mix_weights)
        h = _mix(h, p, attn, mods, l, pool_scale, conv_w, mix_bf, n_rows)
        next_ffn1 = [] if last else [(ffn1_wi, l + 1, 0, HOST_ROWS), (ffn1_wo, l + 1, 0, HOST_ROWS_WO)]
        h, ffn1_bf = _ffn(h, FFN_TM_LATENT if last else FFN_TM, FFN_TF, mods, norm_g, ffn2_wi, ffn2_wo, l, 6,
                          final_g, final=last, hosted=next_ffn1, weights_bf=ffn2_bf)
    return h.reshape(1, SEQ, D_MODEL)
```
